```python
import math
import jax, jax.numpy as jnp
from jax import lax
import numpy as np

D_MODEL = 1024
BATCH = 8
SEQ = 2048
DEPTH = 4

N_MIXERS = 3
N_A = (DEPTH + 2) // 3
N_B = (DEPTH + 1) // 3
N_C = DEPTH // 3

NORM_EPS = 1e-5

RWKV_HEAD = 64
RWKV_HEADS = D_MODEL // RWKV_HEAD
LORA_W = 64
LORA_A = 64
LORA_V = 32
LORA_G = 128
GN_EPS = 64e-5

CONV_WIDTH = 31
CONV_CH = D_MODEL

MLA_HEADS = 16
QK_NOPE = 64
QK_ROPE = 32
V_DIM = 64
Q_LORA = 384
KV_LORA = 256
ROPE_THETA = 10000.0
Q_BLOCK = 128

D_FF = 4 * D_MODEL

kernel_name = "hybrid_rwkv7_conformer_mla_trunk"


def rmsnorm(x, g):
    xf = x.astype(jnp.float32)
    y = xf * lax.rsqrt(jnp.mean(xf * xf, axis=-1, keepdims=True) + NORM_EPS) * g.astype(jnp.float32)
    return y.astype(x.dtype)


def layernorm(x, g, b):
    xf = x.astype(jnp.float32)
    mu = jnp.mean(xf, axis=-1, keepdims=True)
    var = jnp.mean(jnp.square(xf - mu), axis=-1, keepdims=True)
    y = (xf - mu) * lax.rsqrt(var + NORM_EPS) * g.astype(jnp.float32) + b.astype(jnp.float32)
    return y.astype(x.dtype)


def wkv7_scan(r, decay, k, v, a, b):
    B, S, H, N = r.shape

    def step(state, inp):
        r_t, w_t, k_t, v_t, a_t, b_t = inp
        sa = jnp.einsum('bhvk,bhk->bhv', state, a_t)
        state = (state * w_t[:, :, None, :]
                 + sa[..., None] * b_t[:, :, None, :]
                 + v_t[..., None] * k_t[:, :, None, :])
        return state, jnp.einsum('bhvk,bhk->bhv', state, r_t)

    xs = tuple(jnp.moveaxis(t, 1, 0) for t in (r, decay, k, v, a, b))
    s0 = jnp.zeros((B, H, N, N), jnp.float32)
    _, ys = lax.scan(step, s0, xs)
    return jnp.moveaxis(ys, 0, 1)


def rwkv7_time_mix(u, mu, wr, wk, wv, wo, w0, w1, w2, a0, a1, a2, g1, g2,
                   k_k, k_a, r_k, lnx_w, lnx_b, vres):
    B, S, D = u.shape
    H, N = RWKV_HEADS, RWKV_HEAD
    xx = jnp.pad(u, ((0, 0), (1, 0), (0, 0)))[:, :-1] - u
    xr, xw, xk, xv, xa, xg = [u + xx * mu[j] for j in range(6)]
    r = xr @ wr
    k = xk @ wk
    v = xv @ wv
    w = -jax.nn.softplus(-(w0 + jnp.tanh(xw @ w1) @ w2)) - 0.5
    if vres is not None:
        v0, v1, v2, v_first = vres
        v = v + (v_first - v) * jax.nn.sigmoid(v0 + (xv @ v1) @ v2)
    a = jax.nn.sigmoid(a0 + (xa @ a1) @ a2)
    g = jax.nn.sigmoid(xg @ g1) @ g2

    hs = lambda t: t.astype(jnp.float32).reshape(B, S, H, N)
    kk = hs(k * k_k)
    kk = kk * lax.rsqrt(jnp.maximum(jnp.sum(kk * kk, axis=-1, keepdims=True), 1e-24))
    k_eff = k * (1 + (a - 1) * k_a)
    decay = jnp.exp(-jnp.exp(hs(w)))
    rf, kf, vf, af = hs(r), hs(k_eff), hs(v), hs(a)
    y = wkv7_scan(rf, decay, kf, vf, -kk, kk * af)

    mean = jnp.mean(y, axis=-1, keepdims=True)
    var = jnp.mean(jnp.square(y - mean), axis=-1, keepdims=True)
    y = ((y - mean) * lax.rsqrt(var + GN_EPS)).reshape(B, S, D)
    y = y * lnx_w.astype(jnp.float32) + lnx_b.astype(jnp.float32)
    bonus = jnp.sum(rf * kf * r_k.astype(jnp.float32), axis=-1, keepdims=True) * vf
    y = (y + bonus.reshape(B, S, D)) * g.astype(jnp.float32)
    return y.astype(u.dtype) @ wo, v


def conformer_conv(u, w1, b1, dw, bdw, ln_w, ln_b, w2, b2):
    z = u @ w1 + b1
    za, zb = jnp.split(z, 2, axis=-1)
    z = za * jax.nn.sigmoid(zb)
    z = lax.conv_general_dilated(
        z, dw[:, None, :].astype(z.dtype), window_strides=(1,),
        padding=[(CONV_WIDTH - 1, 0)],
        dimension_numbers=('NWC', 'WIO', 'NWC'),
        feature_group_count=CONV_CH) + bdw
    z = jax.nn.silu(layernorm(z, ln_w, ln_b))
    return z @ w2 + b2


def rope_tables(positions):
    inv_freq = ROPE_THETA ** (-jnp.arange(0, QK_ROPE, 2, dtype=jnp.float32) / QK_ROPE)
    ang = positions.astype(jnp.float32)[..., None] * inv_freq
    return jnp.cos(ang), jnp.sin(ang)


def apply_rope(x, cos, sin):
    x1, x2 = jnp.split(x.astype(jnp.float32), 2, axis=-1)
    return jnp.concatenate([x1 * cos - x2 * sin, x1 * sin + x2 * cos], axis=-1).astype(x.dtype)


def mla_attention(u, positions, wd, gq, wuq, gkv, wukv, wo):
    B, S, _ = u.shape
    H = MLA_HEADS
    c = u @ wd
    cq, ckv, kr = jnp.split(c, [Q_LORA, Q_LORA + KV_LORA], axis=-1)
    q = (rmsnorm(cq, gq) @ wuq).reshape(B, S, H, QK_NOPE + QK_ROPE)
    q_nope, q_rope = q[..., :QK_NOPE], q[..., QK_NOPE:]
    kv = (rmsnorm(ckv, gkv) @ wukv).reshape(B, S, H, QK_NOPE + V_DIM)
    k_nope, v = kv[..., :QK_NOPE], kv[..., QK_NOPE:]
    cos, sin = rope_tables(positions)
    q_rope = apply_rope(q_rope, cos[:, :, None, :], sin[:, :, None, :])
    k_rope = apply_rope(kr, cos, sin)
    scale = 1.0 / math.sqrt(QK_NOPE + QK_ROPE)
    key_idx = jnp.arange(S)

    def block(i):
        qs = i * Q_BLOCK
        qn = lax.dynamic_slice_in_dim(q_nope, qs, Q_BLOCK, axis=1)
        qr = lax.dynamic_slice_in_dim(q_rope, qs, Q_BLOCK, axis=1)
        s = (jnp.einsum('bqhd,bkhd->bhqk', qn, k_nope)
             + jnp.einsum('bqhd,bkd->bhqk', qr, k_rope)).astype(jnp.float32) * scale
        q_idx = qs + jnp.arange(Q_BLOCK)
        s = jnp.where(key_idx[None, :] <= q_idx[:, None], s, -1e30)
        p = jax.nn.softmax(s, axis=-1).astype(v.dtype)
        return jnp.einsum('bhqk,bkhd->bqhd', p, v)

    o = lax.map(block, jnp.arange(S // Q_BLOCK))
    o = jnp.moveaxis(o, 0, 1).reshape(B, S, H * V_DIM)
    return o @ wo


def squared_relu_mlp(u, w1, w2):
    return jnp.square(jax.nn.relu(u @ w1)) @ w2


def setup_inputs(seed: int = 0) -> dict:
    key = jax.random.key(seed)
    ks = iter(jax.random.split(key, 64))
    D = D_MODEL

    def nrm(shape, scale):
        return jax.random.normal(next(ks), shape, jnp.float32) * scale

    def gain(shape):
        return 1.0 + nrm(shape, 0.05)

    x = jax.random.normal(next(ks), (BATCH, SEQ, D), jnp.float32)
    start = jax.random.randint(next(ks), (BATCH, 1), 0, 4096, dtype=jnp.int32)
    positions = start + jnp.arange(SEQ, dtype=jnp.int32)[None, :]
    HD = RWKV_HEADS * RWKV_HEAD
    return {
        "x": x,
        "positions": positions,
        "norm_mix": gain((DEPTH, D)),
        "norm_ffn": gain((DEPTH, D)),
        "norm_final": gain((D,)),
        "rw_mu": jax.random.uniform(next(ks), (N_A, 6, D), jnp.float32),
        "rw_wr": nrm((N_A, D, HD), D ** -0.5),
        "rw_wk": nrm((N_A, D, HD), D ** -0.5),
        "rw_wv": nrm((N_A, D, HD), D ** -0.5),
        "rw_wo": nrm((N_A, HD, D), HD ** -0.5),
        "rw_w0": jax.random.uniform(next(ks), (N_A, D), jnp.float32, -6.0, 1.0),
        "rw_w1": nrm((N_A, D, LORA_W), D ** -0.5),
        "rw_w2": nrm((N_A, LORA_W, D), 0.5 * LORA_W ** -0.5),
        "rw_a0": nrm((N_A, D), 0.1),
        "rw_a1": nrm((N_A, D, LORA_A), D ** -0.5),
        "rw_a2": nrm((N_A, LORA_A, D), 0.5 * LORA_A ** -0.5),
        "rw_g1": nrm((N_A, D, LORA_G), D ** -0.5),
        "rw_g2": nrm((N_A, LORA_G, D), LORA_G ** -0.5),
        "rw_kk": 0.85 + nrm((N_A, D), 0.05),
        "rw_ka": 1.0 + nrm((N_A, D), 0.05),
        "rw_rk": nrm((N_A, RWKV_HEADS, RWKV_HEAD), 0.1),
        "rw_lnx_w": gain((N_A, D)),
        "rw_lnx_b": nrm((N_A, D), 0.01),
        "rw_v0": nrm((N_A - 1, D), 0.1),
        "rw_v1": nrm((N_A - 1, D, LORA_V), D ** -0.5),
        "rw_v2": nrm((N_A - 1, LORA_V, D), 0.5 * LORA_V ** -0.5),
        "cv_w1": nrm((N_B, D, 2 * CONV_CH), D ** -0.5),
        "cv_b1": nrm((N_B, 2 * CONV_CH), 0.02),
        "cv_dw": nrm((N_B, CONV_WIDTH, CONV_CH), CONV_WIDTH ** -0.5),
        "cv_bdw": nrm((N_B, CONV_CH), 0.02),
        "cv_ln_w": gain((N_B, CONV_CH)),
        "cv_ln_b": nrm((N_B, CONV_CH), 0.02),
        "cv_w2": nrm((N_B, CONV_CH, D), CONV_CH ** -0.5),
        "cv_b2": nrm((N_B, D), 0.02),
        "ml_wd": nrm((N_C, D, Q_LORA + KV_LORA + QK_ROPE), D ** -0.5),
        "ml_gq": gain((N_C, Q_LORA)),
        "ml_wuq": nrm((N_C, Q_LORA, MLA_HEADS * (QK_NOPE + QK_ROPE)), Q_LORA ** -0.5),
        "ml_gkv": gain((N_C, KV_LORA)),
        "ml_wukv": nrm((N_C, KV_LORA, MLA_HEADS * (QK_NOPE + V_DIM)), KV_LORA ** -0.5),
        "ml_wo": nrm((N_C, MLA_HEADS * V_DIM, D), (MLA_HEADS * V_DIM) ** -0.5),
        "ff_w1": nrm((DEPTH, D, D_FF), D ** -0.5),
        "ff_w2": nrm((DEPTH, D_FF, D), D_FF ** -0.5),
    }


def reference(x, positions, norm_mix, norm_ffn, norm_final,
              rw_mu, rw_wr, rw_wk, rw_wv, rw_wo, rw_w0, rw_w1, rw_w2,
              rw_a0, rw_a1, rw_a2, rw_g1, rw_g2, rw_kk, rw_ka, rw_rk,
              rw_lnx_w, rw_lnx_b, rw_v0, rw_v1, rw_v2,
              cv_w1, cv_b1, cv_dw, cv_bdw, cv_ln_w, cv_ln_b, cv_w2, cv_b2,
              ml_wd, ml_gq, ml_wuq, ml_gkv, ml_wukv, ml_wo,
              ff_w1, ff_w2):
    h = x
    v_first = None
    ia = ib = ic = 0
    for i in range(DEPTH):
        u = rmsnorm(h, norm_mix[i])
        kind = i % N_MIXERS
        if kind == 0:
            vres = None if ia == 0 else (rw_v0[ia - 1], rw_v1[ia - 1], rw_v2[ia - 1], v_first)
            y, v = rwkv7_time_mix(u, rw_mu[ia], rw_wr[ia], rw_wk[ia], rw_wv[ia], rw_wo[ia],
                                  rw_w0[ia], rw_w1[ia], rw_w2[ia], rw_a0[ia], rw_a1[ia], rw_a2[ia],
                                  rw_g1[ia], rw_g2[ia], rw_kk[ia], rw_ka[ia], rw_rk[ia],
                                  rw_lnx_w[ia], rw_lnx_b[ia], vres)
            if ia == 0:
                v_first = v
            ia += 1
        elif kind == 1:
            y = conformer_conv(u, cv_w1[ib], cv_b1[ib], cv_dw[ib], cv_bdw[ib],
                               cv_ln_w[ib], cv_ln_b[ib], cv_w2[ib], cv_b2[ib])
            ib += 1
        else:
            y = mla_attention(u, positions, ml_wd[ic], ml_gq[ic], ml_wuq[ic],
                              ml_gkv[ic], ml_wukv[ic], ml_wo[ic])
            ic += 1
        h = h + y
        h = h + squared_relu_mlp(rmsnorm(h, norm_ffn[i]), ff_w1[i], ff_w2[i])
    return rmsnorm(h, norm_final)
```

```python
import functools
import math

import jax
import jax.numpy as jnp
from jax import lax
from jax.experimental import pallas as pl
from jax.experimental.pallas import tpu as pltpu

F32 = jnp.float32
BF16 = jnp.bfloat16

NORM_EPS = 1e-5
GN_EPS = 64e-5
RWKV_HEAD = 64
CONV_WIDTH = 31
MLA_HEADS = 16
QK_NOPE = 64
QK_ROPE = 32
V_DIM = 64
Q_LORA = 384
KV_LORA = 256
ROPE_THETA = 10000.0

LANES = 128
SUBLANES = 8
MXU_COLS = 256
WKV_CHUNK = 64
VMEM_LIMIT_BYTES = 56 * 1024 * 1024

_NT = (((1,), (1,)), ((), ()))
_TN = (((0,), (0,)), ((), ()))


def _cparams(*sem):
    return pltpu.CompilerParams(dimension_semantics=sem, vmem_limit_bytes=VMEM_LIMIT_BYTES)


def _dot(a, b):
    return jnp.dot(a, b, preferred_element_type=F32)


def _rms(x, g):
    ms = jnp.mean(x * x, axis=-1, keepdims=True)
    return x * lax.rsqrt(ms + NORM_EPS) * g


def _sigmoid(x):
    return 1.0 / (1.0 + jnp.exp(-x))


def _softplus(x):
    return jnp.maximum(x, 0.0) + jnp.log(1.0 + jnp.exp(-jnp.abs(x)))


def _head_sum(x, bd):
    outs = []
    for c in range(x.shape[-1] // MXU_COLS):
        xc = x[:, c * MXU_COLS:(c + 1) * MXU_COLS]
        hi = xc.astype(BF16)
        lo = (xc - hi.astype(F32)).astype(BF16)
        outs.append(_dot(hi, bd) + _dot(lo, bd))
    return jnp.concatenate(outs, axis=-1)


def _full(shape):
    n = len(shape)
    return pl.BlockSpec(shape, lambda *_: (0,) * n)


def _mlp_kernel(final_norm, h_ref, g_ref, w1_ref, w2_ref, gf_ref, o_ref, u_scr, acc_scr):
    k = pl.program_id(1)

    @pl.when(k == 0)
    def _():
        u_scr[...] = _rms(h_ref[...], g_ref[...]).astype(BF16)
        acc_scr[...] = jnp.zeros_like(acc_scr)

    a = _dot(u_scr[...], w1_ref[...])
    a = jnp.square(jnp.maximum(a, 0.0)).astype(BF16)
    acc_scr[...] += _dot(a, w2_ref[...])

    @pl.when(k == pl.num_programs(1) - 1)
    def _():
        out = h_ref[...] + acc_scr[...]
        if final_norm:
            out = _rms(out, gf_ref[...])
        o_ref[...] = out


def _mlp(h2, g, w1, w2, gf, final_norm, tm, tf):
    T, D = h2.shape
    F = w1.shape[1]
    return pl.pallas_call(
        functools.partial(_mlp_kernel, final_norm),
        grid=(T // tm, F // tf),
        in_specs=[
            pl.BlockSpec((tm, D), lambda i, k: (i, 0)),
            pl.BlockSpec((1, D), lambda i, k: (0, 0)),
            pl.BlockSpec((D, tf), lambda i, k: (0, k)),
            pl.BlockSpec((tf, D), lambda i, k: (k, 0)),
            pl.BlockSpec((1, D), lambda i, k: (0, 0)),
        ],
        out_specs=pl.BlockSpec((tm, D), lambda i, k: (i, 0)),
        out_shape=jax.ShapeDtypeStruct((T, D), F32),
        scratch_shapes=[pltpu.VMEM((tm, D), BF16), pltpu.VMEM((tm, D), F32)],
        compiler_params=_cparams("parallel", "arbitrary"),
        name="mlp",
    )(h2, g.reshape(1, D), w1, w2, gf.reshape(1, D))


def _linear_res_kernel(x_ref, w_ref, b_ref, res_ref, o_ref):
    o_ref[...] = res_ref[...] + b_ref[...] + _dot(x_ref[...].astype(BF16), w_ref[...])


def _linear_res(x2, w, bias, res2, tm):
    T, K = x2.shape
    N = w.shape[1]
    return pl.pallas_call(
        _linear_res_kernel,
        grid=(T // tm,),
        in_specs=[
            pl.BlockSpec((tm, K), lambda i: (i, 0)),
            _full((K, N)),
            _full((1, N)),
            pl.BlockSpec((tm, N), lambda i: (i, 0)),
        ],
        out_specs=pl.BlockSpec((tm, N), lambda i: (i, 0)),
        out_shape=jax.ShapeDtypeStruct((T, N), F32),
        compiler_params=_cparams("parallel"),
        name="linear_res",
    )(x2, w, bias.reshape(1, N), res2)


def _rwkv_proj_kernel(has_vres, tm, *refs):
    if has_vres:
        (h_ref, hp_ref, gn_ref, mu_ref, vec_ref, wr_ref, wk_ref, wv_ref, w1_ref, w2_ref, a1_ref, a2_ref,
         g1_ref, g2_ref, bd_ref, v1_ref, v2_ref, vf_ref,
         r_out, lw_out, k_out, v_out, a_out, b_out, g_out) = refs
    else:
        (h_ref, hp_ref, gn_ref, mu_ref, vec_ref, wr_ref, wk_ref, wv_ref, w1_ref, w2_ref, a1_ref, a2_ref,
         g1_ref, g2_ref, bd_ref,
         r_out, lw_out, k_out, v_out, a_out, b_out, g_out) = refs
    i = pl.program_id(1)
    gn = gn_ref[...]
    u = _rms(h_ref[0], gn)
    prev = _rms(hp_ref[0], gn)[SUBLANES - 1:SUBLANES, :]
    prev = jnp.where(i > 0, prev, 0.0)
    row = lax.broadcasted_iota(jnp.int32, (tm, 1), 0)
    shifted = jnp.where(row == 0, prev, pltpu.roll(u, 1, axis=0))
    xx = shifted - u
    mu = mu_ref[...]
    vec = vec_ref[...]
    w0, a0, k_k, k_a = vec[0:1], vec[1:2], vec[2:3], vec[3:4]

    def mix(j):
        return (u + xx * mu[j:j + 1, :]).astype(BF16)

    xr, xw, xk, xv, xa, xg = [mix(j) for j in range(6)]
    r = _dot(xr, wr_ref[...])
    k = _dot(xk, wk_ref[...])
    v = _dot(xv, wv_ref[...])
    wl = w0 + _dot(jnp.tanh(_dot(xw, w1_ref[...])).astype(BF16), w2_ref[...])
    w = -_softplus(-wl) - 0.5
    a = _sigmoid(a0 + _dot(_dot(xa, a1_ref[...]).astype(BF16), a2_ref[...]))
    g = _dot(_sigmoid(_dot(xg, g1_ref[...])).astype(BF16), g2_ref[...])
    if has_vres:
        v0 = vec[4:5]
        gate = _sigmoid(v0 + _dot(_dot(xv, v1_ref[...]).astype(BF16), v2_ref[...]))
        v = v + (vf_ref[0] - v) * gate
    kk = k * k_k
    ss = _head_sum(kk * kk, bd_ref[...])
    kk = kk * lax.rsqrt(jnp.maximum(ss, 1e-24))
    r_out[0] = r
    lw_out[0] = -jnp.exp(w)
    k_out[0] = k * (1.0 + (a - 1.0) * k_a)
    v_out[0] = v
    a_out[0] = -kk
    b_out[0] = kk * a
    g_out[0] = g


def _block_diag_ones(n, blk):
    idx = jnp.arange(n) // blk
    return (idx[:, None] == idx[None, :]).astype(BF16)


def _rwkv_proj(h, gn, mu, vec, ws, vres, tm):
    B, S, D = h.shape
    has_vres = vres is not None
    tok = pl.BlockSpec((1, tm, D), lambda b, i: (b, i, 0))
    prev = pl.BlockSpec((1, SUBLANES, D), lambda b, i: (b, jnp.maximum(i * (tm // SUBLANES) - 1, 0), 0))
    args = [h, h, gn.reshape(1, D), mu, vec] + list(ws) + [_block_diag_ones(MXU_COLS, RWKV_HEAD)]
    specs = [tok, prev, _full((1, D)), _full(mu.shape), _full(vec.shape)] + [_full(w.shape) for w in ws]
    specs.append(_full((MXU_COLS, MXU_COLS)))
    if has_vres:
        v1, v2, v_first = vres
        args += [v1, v2, v_first]
        specs += [_full(v1.shape), _full(v2.shape), tok]
    out = jax.ShapeDtypeStruct((B, S, D), F32)
    return pl.pallas_call(
        functools.partial(_rwkv_proj_kernel, has_vres, tm),
        grid=(B, S // tm),
        in_specs=specs,
        out_specs=[tok] * 7,
        out_shape=[out] * 7,
        compiler_params=_cparams("parallel", "parallel"),
        name="rwkv_proj",
    )(*args)


def _wkv_kernel(r_ref, lw_ref, k_ref, v_ref, a_ref, b_ref, g_ref, vec_ref, tri_ref, bd_ref,
                y_ref, h_scr, y_scr):
    L = WKV_CHUNK
    P = 2 * L
    D = r_ref.shape[-1]

    @pl.when(pl.program_id(1) == 0)
    def _():
        h_scr[...] = jnp.zeros_like(h_scr)

    lw = lw_ref[0]
    cum = jnp.dot(tri_ref[...], lw, precision=lax.Precision.HIGHEST, preferred_element_type=F32)
    e_pos = jnp.exp(cum)
    e_neg = jnp.exp(-cum)
    r = r_ref[0]
    k = k_ref[0]
    v = v_ref[0]
    rt = r * e_pos
    at = a_ref[0] * jnp.exp(cum - lw)
    kt = k * e_neg
    bt = b_ref[0] * e_neg
    w_end = e_pos[L - 1:L, :]
    kl = kt * w_end
    bl = bt * w_end

    lane = lax.broadcasted_iota(jnp.int32, (1, LANES), 1)
    first = lane < RWKV_HEAD
    row = lax.broadcasted_iota(jnp.int32, (P, P), 0)
    col = lax.broadcasted_iota(jnp.int32, (P, P), 1)
    strict = row > col
    incl = row >= col
    eye = (row == col).astype(F32)

    def stack(x):
        return jnp.concatenate([jnp.where(first, x, 0.0), jnp.where(first, 0.0, x)], axis=0).astype(BF16)

    for p in range(D // LANES):
        sl = slice(p * LANES, (p + 1) * LANES)
        a_s, r_s, b_s, k_s = stack(at[:, sl]), stack(rt[:, sl]), stack(bt[:, sl]), stack(kt[:, sl])
        v_s, bl_s, kl_s = stack(v[:, sl]), stack(bl[:, sl]), stack(kl[:, sl])
        m_ab = jnp.where(strict, lax.dot_general(a_s, b_s, _NT, preferred_element_type=F32), 0.0)
        m_ak = jnp.where(strict, lax.dot_general(a_s, k_s, _NT, preferred_element_type=F32), 0.0)
        m_rb = jnp.where(incl, lax.dot_general(r_s, b_s, _NT, preferred_element_type=F32), 0.0)
        m_rk = jnp.where(incl, lax.dot_general(r_s, k_s, _NT, preferred_element_type=F32), 0.0)
        t_inv = eye + m_ab
        a_pow = m_ab.astype(BF16)
        n = 2
        while n < L + 1:
            a_sq = _dot(a_pow, a_pow)
            a_pow = a_sq.astype(BF16)
            t_inv = t_inv + _dot(t_inv.astype(BF16), a_pow)
            n *= 2
        h0 = h_scr[p]
        h0b = h0.astype(BF16)
        x = _dot(a_s, h0b) + _dot(m_ak.astype(BF16), v_s)
        u_s = _dot(t_inv.astype(BF16), x.astype(BF16)).astype(BF16)
        y = _dot(r_s, h0b) + _dot(m_rb.astype(BF16), u_s) + _dot(m_rk.astype(BF16), v_s)
        y_scr[:, sl] = y[:L] + y[L:]
        dh = (lax.dot_general(bl_s, u_s, _TN, preferred_element_type=F32)
              + lax.dot_general(kl_s, v_s, _TN, preferred_element_type=F32))
        decay = jnp.transpose(jnp.broadcast_to(w_end[:, sl], (LANES, LANES)))
        h_scr[p] = h0 * decay + dh

    vec = vec_ref[...]
    r_k, ln_w, ln_b = vec[0:1], vec[1:2], vec[2:3]
    bd = bd_ref[...]
    y = y_scr[...]
    inv_n = 1.0 / RWKV_HEAD
    d = y - _head_sum(y, bd) * inv_n
    var = _head_sum(d * d, bd) * inv_n
    yn = d * lax.rsqrt(var + GN_EPS) * ln_w + ln_b
    bonus = _head_sum(r * k * r_k, bd) * v
    y_ref[0] = ((yn + bonus) * g_ref[0]).astype(y_ref.dtype)


def _wkv(r, lw, k, v, a, b, g, vec):
    B, S, D = r.shape
    L = WKV_CHUNK
    tok = pl.BlockSpec((1, L, D), lambda bi, c: (bi, c, 0))
    tri = (jnp.arange(L)[:, None] >= jnp.arange(L)[None, :]).astype(F32)
    return pl.pallas_call(
        _wkv_kernel,
        grid=(B, S // L),
        in_specs=[tok] * 7 + [_full(vec.shape), _full((L, L)), _full((MXU_COLS, MXU_COLS))],
        out_specs=tok,
        out_shape=jax.ShapeDtypeStruct((B, S, D), BF16),
        scratch_shapes=[pltpu.VMEM((D // LANES, LANES, LANES), F32), pltpu.VMEM((L, D), F32)],
        compiler_params=_cparams("parallel", "arbitrary"),
        name="wkv",
    )(r, lw, k, v, a, b, g, vec, tri, _block_diag_ones(MXU_COLS, RWKV_HEAD))


def _rwkv_layer(h, gn, mu, wr, wk, wv, wo, w0, w1, w2, a0, a1, a2, g1, g2, k_k, k_a, r_k, lnx_w, lnx_b,
                vres, tm):
    B, S, D = h.shape
    rows = [w0, a0, k_k, k_a]
    extra = None
    if vres is not None:
        v0, v1, v2, v_first = vres
        rows.append(v0)
        extra = (v1.astype(BF16), v2.astype(BF16), v_first)
    vec = jnp.stack(rows + [jnp.zeros_like(w0)] * (SUBLANES - len(rows)))
    mu8 = jnp.concatenate([mu, jnp.zeros((SUBLANES - mu.shape[0], D), F32)])
    ws = [w.astype(BF16) for w in (wr, wk, wv, w1, w2, a1, a2, g1, g2)]
    r, lw, k, v, a, b, g = _rwkv_proj(h, gn, mu8, vec, ws, extra, tm)
    vec2 = jnp.stack([r_k.reshape(D), lnx_w, lnx_b] + [jnp.zeros_like(w0)] * (SUBLANES - 3))
    y = _wkv(r, lw, k, v, a, b, g, vec2)
    out = _linear_res(y.reshape(B * S, D), wo.astype(BF16), jnp.zeros((D,), F32), h.reshape(B * S, D), tm)
    return out.reshape(B, S, D), v


def _glu_kernel(h_ref, g_ref, w_ref, b_ref, z_ref):
    C = z_ref.shape[-1]
    u = _rms(h_ref[...], g_ref[...]).astype(BF16)
    z = _dot(u, w_ref[...]) + b_ref[...]
    z_ref[...] = z[:, :C] * _sigmoid(z[:, C:])


def _glu(h2, g, w1, b1, tm):
    T, D = h2.shape
    N = w1.shape[1]
    return pl.pallas_call(
        _glu_kernel,
        grid=(T // tm,),
        in_specs=[pl.BlockSpec((tm, D), lambda i: (i, 0)), _full((1, D)), _full((D, N)), _full((1, N))],
        out_specs=pl.BlockSpec((tm, N // 2), lambda i: (i, 0)),
        out_shape=jax.ShapeDtypeStruct((T, N // 2), F32),
        compiler_params=_cparams("parallel"),
        name="conv_glu",
    )(h2, g.reshape(1, D), w1, b1.reshape(1, N))


_CONV_HALO = 32


def _dwconv_kernel(tm, z_ref, zp_ref, h_ref, dw_ref, vec_ref, w2_ref, o_ref, zbuf):
    i = pl.program_id(1)
    zbuf[0:_CONV_HALO, :] = jnp.where(i > 0, zp_ref[0], 0.0)
    zbuf[_CONV_HALO:, :] = z_ref[0]
    dw = dw_ref[...]
    vec = vec_ref[...]
    bdw, ln_w, ln_b, b2 = vec[0:1], vec[1:2], vec[2:3], vec[3:4]
    off = _CONV_HALO - (CONV_WIDTH - 1)
    acc = zbuf[pl.ds(off, tm), :] * dw[0:1, :]
    for j in range(1, CONV_WIDTH):
        acc = acc + zbuf[pl.ds(off + j, tm), :] * dw[j:j + 1, :]
    acc = acc + bdw
    mean = jnp.mean(acc, axis=-1, keepdims=True)
    d = acc - mean
    var = jnp.mean(d * d, axis=-1, keepdims=True)
    y = d * lax.rsqrt(var + NORM_EPS) * ln_w + ln_b
    y = y * _sigmoid(y)
    o_ref[0] = h_ref[0] + b2 + _dot(y.astype(BF16), w2_ref[...])


def _dwconv(z, h, dw, vec, w2, tm):
    B, S, C = z.shape
    D = w2.shape[1]
    tok = pl.BlockSpec((1, tm, C), lambda b, i: (b, i, 0))
    prev = pl.BlockSpec((1, _CONV_HALO, C), lambda b, i: (b, jnp.maximum(i * (tm // _CONV_HALO) - 1, 0), 0))
    return pl.pallas_call(
        functools.partial(_dwconv_kernel, tm),
        grid=(B, S // tm),
        in_specs=[tok, prev, pl.BlockSpec((1, tm, D), lambda b, i: (b, i, 0)),
                  _full(dw.shape), _full(vec.shape), _full(w2.shape)],
        out_specs=pl.BlockSpec((1, tm, D), lambda b, i: (b, i, 0)),
        out_shape=jax.ShapeDtypeStruct((B, S, D), F32),
        scratch_shapes=[pltpu.VMEM((_CONV_HALO + tm, C), F32)],
        compiler_params=_cparams("parallel", "parallel"),
        name="conv_dw",
    )(z, z, h, dw, vec, w2)


def _conv_layer(h, gn, w1, b1, dw, bdw, ln_w, ln_b, w2, b2, tm):
    B, S, D = h.shape
    z = _glu(h.reshape(B * S, D), gn, w1.astype(BF16), b1, tm).reshape(B, S, -1)
    dw32 = jnp.concatenate([dw, jnp.zeros((32 - CONV_WIDTH, dw.shape[1]), F32)])
    vec = jnp.stack([bdw, ln_w, ln_b, b2] + [jnp.zeros_like(b2)] * (SUBLANES - 4))
    return _dwconv(z, h, dw32, vec, w2.astype(BF16), tm)


def _rope_table_kernel(pos_ref, frq_ref, cos_ref, sin_ref):
    ang = pos_ref[...] * frq_ref[...]
    cos_ref[...] = jnp.cos(ang)
    sin_ref[...] = jnp.sin(ang)


def _rope_tables(positions):
    B, S = positions.shape
    half = QK_ROPE // 2
    per_row = LANES // half
    rows = B * S // per_row
    inv_freq = ROPE_THETA ** (-jnp.arange(0, QK_ROPE, 2, dtype=F32) / QK_ROPE)
    pos = jnp.repeat(positions.astype(F32).reshape(rows, per_row), half, axis=1)
    frq = jnp.tile(inv_freq, per_row).reshape(1, LANES)
    tr = min(rows, 512)
    cos, sin = pl.pallas_call(
        _rope_table_kernel,
        grid=(rows // tr,),
        in_specs=[pl.BlockSpec((tr, LANES), lambda i: (i, 0)), _full((1, LANES))],
        out_specs=[pl.BlockSpec((tr, LANES), lambda i: (i, 0))] * 2,
        out_shape=[jax.ShapeDtypeStruct((rows, LANES), F32)] * 2,
        compiler_params=_cparams("parallel"),
        name="rope_table",
    )(pos, frq)
    cos = cos.reshape(B * S, half)
    sin = sin.reshape(B * S, half)
    cos_t = jnp.tile(cos, (1, LANES // half))
    sin_t = jnp.tile(jnp.concatenate([-sin, sin], axis=1), (1, LANES // QK_ROPE))
    return cos_t, sin_t


def _mla_proj_kernel(scale, h_ref, gn_ref, wd_ref, gq_ref, gkv_ref, wqn_ref, wqr_ref, wqs_ref, wkn_ref, wv_ref,
                     cos_ref, sin_ref, qn_out, qr_out, kn_out, kr_out, v_out):
    u = _rms(h_ref[...], gn_ref[...]).astype(BF16)
    c = _dot(u, wd_ref[...])
    cq = _rms(c[:, :Q_LORA], gq_ref[...]).astype(BF16)
    ckv = _rms(c[:, Q_LORA:Q_LORA + KV_LORA], gkv_ref[...]).astype(BF16)
    o = Q_LORA + KV_LORA
    cos = cos_ref[...]
    sin = sin_ref[...]
    kr_out[...] = (c[:, o:o + LANES] * cos + c[:, o + LANES:o + 2 * LANES] * sin).astype(BF16)
    reps = qr_out.shape[-1] // LANES
    cos_w = jnp.concatenate([cos] * reps, axis=-1)
    sin_w = jnp.concatenate([sin] * reps, axis=-1)
    qn_out[...] = (_dot(cq, wqn_ref[...]) * scale).astype(BF16)
    qr = _dot(cq, wqr_ref[...]) * cos_w + _dot(cq, wqs_ref[...]) * sin_w
    qr_out[...] = (qr * scale).astype(BF16)
    kn_out[...] = _dot(ckv, wkn_ref[...]).astype(BF16)
    v_out[...] = _dot(ckv, wv_ref[...]).astype(BF16)


def _swap_halves(w):
    half = QK_ROPE // 2
    return jnp.concatenate([w[..., half:], w[..., :half]], axis=-1)


def _mla_proj(h2, gn, wd, gq, wuq, gkv, wukv, cos_t, sin_t, tm):
    T, D = h2.shape
    H = MLA_HEADS
    npair = H // 2
    HD = H * QK_NOPE
    kr_w = wd[:, Q_LORA + KV_LORA:]
    zpad = jnp.zeros((D, LANES - 2 * QK_ROPE), F32)
    wd2 = jnp.concatenate([wd[:, :Q_LORA + KV_LORA], kr_w, kr_w, zpad,
                           _swap_halves(kr_w), _swap_halves(kr_w), zpad], axis=1).astype(BF16)
    wq = wuq.reshape(Q_LORA, H, QK_NOPE + QK_ROPE)
    wqn = wq[:, :, :QK_NOPE].reshape(Q_LORA, HD).astype(BF16)
    wq_rope = wq[:, :, QK_NOPE:]

    def pair_pad(w):
        w = w.reshape(Q_LORA, npair, 2 * QK_ROPE)
        w = jnp.concatenate([w, jnp.zeros((Q_LORA, npair, LANES - 2 * QK_ROPE), F32)], axis=-1)
        return w.reshape(Q_LORA, npair * LANES).astype(BF16)

    wqr = pair_pad(wq_rope)
    wqs = pair_pad(_swap_halves(wq_rope))
    wkv = wukv.reshape(KV_LORA, H, QK_NOPE + V_DIM)
    wkn = wkv[:, :, :QK_NOPE].reshape(KV_LORA, HD).astype(BF16)
    wv = wkv[:, :, QK_NOPE:].reshape(KV_LORA, H * V_DIM).astype(BF16)
    scale = 1.0 / math.sqrt(QK_NOPE + QK_ROPE)
    ws = [wd2, gq.reshape(1, -1), gkv.reshape(1, -1), wqn, wqr, wqs, wkn, wv]
    tokw = lambda n: pl.BlockSpec((tm, n), lambda i: (i, 0))
    return pl.pallas_call(
        functools.partial(_mla_proj_kernel, scale),
        grid=(T // tm,),
        in_specs=[tokw(D), _full((1, D))] + [_full(w.shape) for w in ws] + [tokw(LANES), tokw(LANES)],
        out_specs=[tokw(HD), tokw(npair * LANES), tokw(HD), tokw(LANES), tokw(H * V_DIM)],
        out_shape=[jax.ShapeDtypeStruct((T, HD), BF16), jax.ShapeDtypeStruct((T, npair * LANES), BF16),
                   jax.ShapeDtypeStruct((T, HD), BF16), jax.ShapeDtypeStruct((T, LANES), BF16),
                   jax.ShapeDtypeStruct((T, H * V_DIM), BF16)],
        compiler_params=_cparams("parallel"),
        name="mla_proj",
    )(h2, gn.reshape(1, D), *ws, cos_t, sin_t)


def _attn_kernel(tq, qn_ref, qr_ref, kn_ref, kr_ref, v_ref, o_ref):
    i = pl.program_id(2)
    qc = jnp.concatenate([qn_ref[0], qr_ref[0]], axis=-1)
    lane = lax.broadcasted_iota(jnp.int32, (1, 2 * LANES), 1)
    head0 = (lane < QK_NOPE) | ((lane >= LANES) & (lane < LANES + QK_ROPE))
    head1 = ((lane >= QK_NOPE) & (lane < LANES)) | ((lane >= LANES + QK_ROPE) & (lane < LANES + 2 * QK_ROPE))
    zero = jnp.zeros_like(qc)
    qs = jnp.concatenate([jnp.where(head0, qc, zero), jnp.where(head1, qc, zero)], axis=0)

    def scores(j):
        kc = jnp.concatenate([kn_ref[0, pl.ds(j * tq, tq), :], kr_ref[0, pl.ds(j * tq, tq), :]], axis=-1)
        return lax.dot_general(qs, kc, _NT, preferred_element_type=F32)

    def update(j, s, carry):
        m, l, acc = carry
        m_new = jnp.maximum(m, jnp.max(s, axis=-1, keepdims=True))
        alpha = jnp.exp(m - m_new)
        p = jnp.exp(s - m_new)
        l = alpha * l + jnp.sum(p, axis=-1, keepdims=True)
        acc = alpha * acc + _dot(p.astype(BF16), v_ref[0, pl.ds(j * tq, tq), :])
        return m_new, l, acc

    def body(j, carry):
        return update(j, scores(j), carry)

    init = (jnp.full((2 * tq, 1), -1e30, F32), jnp.zeros((2 * tq, 1), F32), jnp.zeros((2 * tq, LANES), F32))
    carry = lax.fori_loop(0, i, body, init)
    row = lax.broadcasted_iota(jnp.int32, (2 * tq, tq), 0) % tq
    col = lax.broadcasted_iota(jnp.int32, (2 * tq, tq), 1)
    s = jnp.where(col <= row, scores(i), -1e30)
    _, l, acc = update(i, s, carry)
    o = acc / l
    out_lane = lax.broadcasted_iota(jnp.int32, (1, LANES), 1)
    o_ref[0] = jnp.where(out_lane < V_DIM, o[:tq], o[tq:]).astype(o_ref.dtype)


def _attention(qn, qr, kn, kr, v, tq):
    B, S, HD = qn.shape
    npair = HD // LANES
    qspec = pl.BlockSpec((1, tq, LANES), lambda b, p, i: (b, i, p))
    kspec = pl.BlockSpec((1, S, LANES), lambda b, p, i: (b, 0, p))
    return pl.pallas_call(
        functools.partial(_attn_kernel, tq),
        grid=(B, npair, S // tq),
        in_specs=[qspec, qspec, kspec, pl.BlockSpec((1, S, LANES), lambda b, p, i: (b, 0, 0)), kspec],
        out_specs=qspec,
        out_shape=jax.ShapeDtypeStruct((B, S, HD), BF16),
        compiler_params=_cparams("parallel", "parallel", "arbitrary"),
        name="mla_attn",
    )(qn, qr, kn, kr, v)


def _mla_layer(h, positions, gn, wd, gq, wuq, gkv, wukv, wo, tm, tq):
    B, S, D = h.shape
    cos_t, sin_t = _rope_tables(positions)
    qn, qr, kn, kr, v = _mla_proj(h.reshape(B * S, D), gn, wd, gq, wuq, gkv, wukv, cos_t, sin_t, tm)
    sh = lambda t: t.reshape(B, S, t.shape[-1])
    o = _attention(sh(qn), sh(qr), sh(kn), sh(kr), sh(v), tq)
    out = _linear_res(o.reshape(B * S, -1), wo.astype(BF16), jnp.zeros((D,), F32), h.reshape(B * S, D), tm)
    return out.reshape(B, S, D)


def _tile(n, pref):
    t = min(n, pref)
    while n % t:
        t //= 2
    return t


def kernel(x, positions, norm_mix, norm_ffn, norm_final, rw_mu, rw_wr, rw_wk, rw_wv, rw_wo, rw_w0, rw_w1, rw_w2, rw_a0, rw_a1, rw_a2, rw_g1, rw_g2, rw_kk, rw_ka, rw_rk, rw_lnx_w, rw_lnx_b, rw_v0, rw_v1, rw_v2, cv_w1, cv_b1, cv_dw, cv_bdw, cv_ln_w, cv_ln_b, cv_w2, cv_b2, ml_wd, ml_gq, ml_wuq, ml_gkv, ml_wukv, ml_wo, ff_w1, ff_w2):
    B, S, D = x.shape
    depth = norm_mix.shape[0]
    tm = _tile(S, 512)
    tq = _tile(S, 256)
    tf = _tile(ff_w1.shape[-1], 1024)
    h = x
    v_first = None
    ia = ib = ic = 0
    for i in range(depth):
        kind = i % 3
        if kind == 0:
            vres = None if ia == 0 else (rw_v0[ia - 1], rw_v1[ia - 1], rw_v2[ia - 1], v_first)
            h, v = _rwkv_layer(h, norm_mix[i], rw_mu[ia], rw_wr[ia], rw_wk[ia], rw_wv[ia], rw_wo[ia],
                               rw_w0[ia], rw_w1[ia], rw_w2[ia], rw_a0[ia], rw_a1[ia], rw_a2[ia],
                               rw_g1[ia], rw_g2[ia], rw_kk[ia], rw_ka[ia], rw_rk[ia],
                               rw_lnx_w[ia], rw_lnx_b[ia], vres, tm)
            if ia == 0:
                v_first = v
            ia += 1
        elif kind == 1:
            h = _conv_layer(h, norm_mix[i], cv_w1[ib], cv_b1[ib], cv_dw[ib], cv_bdw[ib],
                            cv_ln_w[ib], cv_ln_b[ib], cv_w2[ib], cv_b2[ib], tm)
            ib += 1
        else:
            h = _mla_layer(h, positions, norm_mix[i], ml_wd[ic], ml_gq[ic], ml_wuq[ic],
                           ml_gkv[ic], ml_wukv[ic], ml_wo[ic], tm, tq)
            ic += 1
        h = _mlp(h.reshape(B * S, D), norm_ffn[i], ff_w1[i].astype(BF16), ff_w2[i].astype(BF16),
                 norm_final, i == depth - 1, tm, tf).reshape(B, S, D)
    return h
```

```python
import functools
import math

import jax
import jax.numpy as jnp
from jax import lax
from jax.experimental import pallas as pl
from jax.experimental.pallas import tpu as pltpu

F32 = jnp.float32
BF16 = jnp.bfloat16

NORM_EPS = 1e-5
GN_EPS = 64e-5
RWKV_HEAD = 64
CONV_WIDTH = 31
MLA_HEADS = 16
QK_NOPE = 64
QK_ROPE = 32
V_DIM = 64
Q_LORA = 384
KV_LORA = 256
ROPE_THETA = 10000.0

LANES = 128
SUBLANES = 8
MXU_COLS = 256
WKV_CHUNK = 64
VMEM_LIMIT_BYTES = 56 * 1024 * 1024

_NT = (((1,), (1,)), ((), ()))
_TN = (((0,), (0,)), ((), ()))


def _cparams(*sem):
    return pltpu.CompilerParams(dimension_semantics=sem, vmem_limit_bytes=VMEM_LIMIT_BYTES)


def _dot(a, b):
    return jnp.dot(a, b, preferred_element_type=F32)


def _rms(x, g):
    ms = jnp.mean(x * x, axis=-1, keepdims=True)
    return x * lax.rsqrt(ms + NORM_EPS) * g


def _sigmoid(x):
    return 1.0 / (1.0 + jnp.exp(-x))


def _softplus(x):
    return jnp.maximum(x, 0.0) + jnp.log(1.0 + jnp.exp(-jnp.abs(x)))


def _head_sum(x, bd):
    outs = []
    for c in range(x.shape[-1] // MXU_COLS):
        xc = x[:, c * MXU_COLS:(c + 1) * MXU_COLS]
        hi = xc.astype(BF16)
        lo = (xc - hi.astype(F32)).astype(BF16)
        outs.append(_dot(hi, bd) + _dot(lo, bd))
    return jnp.concatenate(outs, axis=-1)


def _full(shape):
    n = len(shape)
    return pl.BlockSpec(shape, lambda *_: (0,) * n)


def _mlp_kernel(final_norm, tf, h_ref, g_ref, w1_ref, w2_ref, gf_ref, o_ref, a_scr):
    h = h_ref[...]
    u = _rms(h, g_ref[...]).astype(BF16)
    for c in range(a_scr.shape[1] // tf):
        cols = slice(c * tf, (c + 1) * tf)
        a = _dot(u, w1_ref[:, cols])
        a_scr[:, cols] = jnp.square(jnp.maximum(a, 0.0)).astype(BF16)
    out = h + _dot(a_scr[...], w2_ref[...])
    if final_norm:
        out = _rms(out, gf_ref[...])
    o_ref[...] = out


def _resident(shape):
    n = len(shape)
    return pl.BlockSpec(shape, lambda *_: (0,) * n, pipeline_mode=pl.Buffered(1))


def _mlp(h2, g, w1, w2, gf, final_norm, tm, tf):
    T, D = h2.shape
    F = w1.shape[1]
    return pl.pallas_call(
        functools.partial(_mlp_kernel, final_norm, tf),
        grid=(T // tm,),
        in_specs=[
            pl.BlockSpec((tm, D), lambda i: (i, 0)),
            _full((1, D)),
            _resident((D, F)),
            _resident((F, D)),
            _full((1, D)),
        ],
        out_specs=pl.BlockSpec((tm, D), lambda i: (i, 0)),
        out_shape=jax.ShapeDtypeStruct((T, D), F32),
        scratch_shapes=[pltpu.VMEM((tm, F), BF16)],
        compiler_params=_cparams("parallel"),
        name="mlp",
    )(h2, g.reshape(1, D), w1, w2, gf.reshape(1, D))


def _linear_res_kernel(x_ref, w_ref, b_ref, res_ref, o_ref):
    o_ref[...] = res_ref[...] + b_ref[...] + _dot(x_ref[...].astype(BF16), w_ref[...])


def _linear_res(x2, w, bias, res2, tm):
    T, K = x2.shape
    N = w.shape[1]
    return pl.pallas_call(
        _linear_res_kernel,
        grid=(T // tm,),
        in_specs=[
            pl.BlockSpec((tm, K), lambda i: (i, 0)),
            _full((K, N)),
            _full((1, N)),
            pl.BlockSpec((tm, N), lambda i: (i, 0)),
        ],
        out_specs=pl.BlockSpec((tm, N), lambda i: (i, 0)),
        out_shape=jax.ShapeDtypeStruct((T, N), F32),
        compiler_params=_cparams("parallel"),
        name="linear_res",
    )(x2, w, bias.reshape(1, N), res2)


def _rwkv_proj_kernel(has_vres, tm, *refs):
    if has_vres:
        (h_ref, hp_ref, gn_ref, mu_ref, vec_ref, wr_ref, wk_ref, wv_ref, w1_ref, w2_ref, a1_ref, a2_ref,
         g1_ref, g2_ref, bd_ref, v1_ref, v2_ref, vf_ref,
         r_out, lw_out, k_out, v_out, a_out, b_out, g_out) = refs
    else:
        (h_ref, hp_ref, gn_ref, mu_ref, vec_ref, wr_ref, wk_ref, wv_ref, w1_ref, w2_ref, a1_ref, a2_ref,
         g1_ref, g2_ref, bd_ref,
         r_out, lw_out, k_out, v_out, a_out, b_out, g_out) = refs
    i = pl.program_id(1)
    gn = gn_ref[...]
    u = _rms(h_ref[0], gn)
    prev = _rms(hp_ref[0], gn)[SUBLANES - 1:SUBLANES, :]
    prev = jnp.where(i > 0, prev, 0.0)
    row = lax.broadcasted_iota(jnp.int32, (tm, 1), 0)
    shifted = jnp.where(row == 0, prev, pltpu.roll(u, 1, axis=0))
    xx = shifted - u
    mu = mu_ref[...]
    vec = vec_ref[...]
    w0, a0, k_k, k_a = vec[0:1], vec[1:2], vec[2:3], vec[3:4]

    def mix(j):
        return (u + xx * mu[j:j + 1, :]).astype(BF16)

    xr, xw, xk, xv, xa, xg = [mix(j) for j in range(6)]
    r = _dot(xr, wr_ref[...])
    k = _dot(xk, wk_ref[...])
    v = _dot(xv, wv_ref[...])
    wl = w0 + _dot(jnp.tanh(_dot(xw, w1_ref[...])).astype(BF16), w2_ref[...])
    w = -_softplus(-wl) - 0.5
    a = _sigmoid(a0 + _dot(_dot(xa, a1_ref[...]).astype(BF16), a2_ref[...]))
    g = _dot(_sigmoid(_dot(xg, g1_ref[...])).astype(BF16), g2_ref[...])
    if has_vres:
        v0 = vec[4:5]
        gate = _sigmoid(v0 + _dot(_dot(xv, v1_ref[...]).astype(BF16), v2_ref[...]))
        v = v + (vf_ref[0] - v) * gate
    kk = k * k_k
    ss = _head_sum(kk * kk, bd_ref[...])
    kk = kk * lax.rsqrt(jnp.maximum(ss, 1e-24))
    r_out[0] = r
    lw_out[0] = -jnp.exp(w)
    k_out[0] = k * (1.0 + (a - 1.0) * k_a)
    v_out[0] = v
    a_out[0] = -kk
    b_out[0] = kk * a
    g_out[0] = g


def _block_diag_ones(n, blk):
    idx = jnp.arange(n) // blk
    return (idx[:, None] == idx[None, :]).astype(BF16)


def _rwkv_proj(h, gn, mu, vec, ws, vres, tm):
    B, S, D = h.shape
    has_vres = vres is not None
    tok = pl.BlockSpec((1, tm, D), lambda b, i: (b, i, 0))
    prev = pl.BlockSpec((1, SUBLANES, D), lambda b, i: (b, jnp.maximum(i * (tm // SUBLANES) - 1, 0), 0))
    args = [h, h, gn.reshape(1, D), mu, vec] + list(ws) + [_block_diag_ones(MXU_COLS, RWKV_HEAD)]
    specs = [tok, prev, _full((1, D)), _full(mu.shape), _full(vec.shape)] + [_full(w.shape) for w in ws]
    specs.append(_full((MXU_COLS, MXU_COLS)))
    if has_vres:
        v1, v2, v_first = vres
        args += [v1, v2, v_first]
        specs += [_full(v1.shape), _full(v2.shape), tok]
    out = jax.ShapeDtypeStruct((B, S, D), F32)
    return pl.pallas_call(
        functools.partial(_rwkv_proj_kernel, has_vres, tm),
        grid=(B, S // tm),
        in_specs=specs,
        out_specs=[tok] * 7,
        out_shape=[out] * 7,
        compiler_params=_cparams("parallel", "parallel"),
        name="rwkv_proj",
    )(*args)


def _wkv_kernel(r_ref, lw_ref, k_ref, v_ref, a_ref, b_ref, g_ref, vec_ref, tri_ref, bd_ref,
                y_ref, h_scr, y_scr):
    L = WKV_CHUNK
    P = 2 * L
    D = r_ref.shape[-1]

    @pl.when(pl.program_id(1) == 0)
    def _():
        h_scr[...] = jnp.zeros_like(h_scr)

    lw = lw_ref[0]
    cum = jnp.dot(tri_ref[...], lw, precision=lax.Precision.HIGHEST, preferred_element_type=F32)
    e_pos = jnp.exp(cum)
    e_neg = jnp.exp(-cum)
    r = r_ref[0]
    k = k_ref[0]
    v = v_ref[0]
    rt = r * e_pos
    at = a_ref[0] * jnp.exp(cum - lw)
    kt = k * e_neg
    bt = b_ref[0] * e_neg
    w_end = e_pos[L - 1:L, :]
    kl = kt * w_end
    bl = bt * w_end

    lane = lax.broadcasted_iota(jnp.int32, (1, LANES), 1)
    first = lane < RWKV_HEAD
    row = lax.broadcasted_iota(jnp.int32, (P, P), 0)
    col = lax.broadcasted_iota(jnp.int32, (P, P), 1)
    strict = row > col
    incl = row >= col
    eye = (row == col).astype(F32)

    def stack(x):
        return jnp.concatenate([jnp.where(first, x, 0.0), jnp.where(first, 0.0, x)], axis=0).astype(BF16)

    def stack2(x, y, sl):
        return jnp.concatenate([stack(x[:, sl]), stack(y[:, sl])], axis=0)

    pairs = [slice(p * LANES, (p + 1) * LANES) for p in range(D // LANES)]
    ar = [stack2(at, rt, sl) for sl in pairs]
    bk = [stack2(bt, kt, sl) for sl in pairs]
    blkl = [stack2(bl, kl, sl) for sl in pairs]
    v_s = [stack(v[:, sl]) for sl in pairs]
    h0 = [h_scr[p] for p in range(len(pairs))]
    sc = [lax.dot_general(x, y, _NT, preferred_element_type=F32) for x, y in zip(ar, bk)]
    m_ab = [jnp.where(strict, s[:P, :P], 0.0) for s in sc]
    m_rb = [jnp.where(incl, s[P:, :P], 0.0).astype(BF16) for s in sc]
    m_k = [jnp.concatenate([jnp.where(strict, s[:P, P:], 0.0), jnp.where(incl, s[P:, P:], 0.0)],
                           axis=0).astype(BF16) for s in sc]
    mkv = [_dot(m, vs) for m, vs in zip(m_k, v_s)]
    xh = [_dot(x, h.astype(BF16)) for x, h in zip(ar, h0)]
    t_inv = [eye + m for m in m_ab]
    a_pow = [m.astype(BF16) for m in m_ab]
    n = 2
    while n < L + 1:
        a_pow = [_dot(m, m).astype(BF16) for m in a_pow]
        t_inv = [t + _dot(t.astype(BF16), m) for t, m in zip(t_inv, a_pow)]
        n *= 2
    u_s = [_dot(t.astype(BF16), (xa[:P] + xb[:P]).astype(BF16)).astype(BF16)
           for t, xa, xb in zip(t_inv, xh, mkv)]
    ys = [xa[P:] + xb[P:] + _dot(m, u) for xa, xb, m, u in zip(xh, mkv, m_rb, u_s)]
    dh = [lax.dot_general(w, jnp.concatenate([u, vs], axis=0), _TN, preferred_element_type=F32)
          for w, u, vs in zip(blkl, u_s, v_s)]
    for p, sl in enumerate(pairs):
        y_scr[:, sl] = ys[p][:L] + ys[p][L:]
        decay = jnp.transpose(jnp.broadcast_to(w_end[:, sl], (LANES, LANES)))
        h_scr[p] = h0[p] * decay + dh[p]

    vec = vec_ref[...]
    r_k, ln_w, ln_b = vec[0:1], vec[1:2], vec[2:3]
    bd = bd_ref[...]
    y = y_scr[...]
    inv_n = 1.0 / RWKV_HEAD
    d = y - _head_sum(y, bd) * inv_n
    var = _head_sum(d * d, bd) * inv_n
    yn = d * lax.rsqrt(var + GN_EPS) * ln_w + ln_b
    bonus = _head_sum(r * k * r_k, bd) * v
    y_ref[0] = ((yn + bonus) * g_ref[0]).astype(y_ref.dtype)


def _wkv(r, lw, k, v, a, b, g, vec):
    B, S, D = r.shape
    L = WKV_CHUNK
    tok = pl.BlockSpec((1, L, D), lambda bi, c: (bi, c, 0))
    tri = (jnp.arange(L)[:, None] >= jnp.arange(L)[None, :]).astype(F32)
    return pl.pallas_call(
        _wkv_kernel,
        grid=(B, S // L),
        in_specs=[tok] * 7 + [_full(vec.shape), _full((L, L)), _full((MXU_COLS, MXU_COLS))],
        out_specs=tok,
        out_shape=jax.ShapeDtypeStruct((B, S, D), BF16),
        scratch_shapes=[pltpu.VMEM((D // LANES, LANES, LANES), F32), pltpu.VMEM((L, D), F32)],
        compiler_params=_cparams("parallel", "arbitrary"),
        name="wkv",
    )(r, lw, k, v, a, b, g, vec, tri, _block_diag_ones(MXU_COLS, RWKV_HEAD))


def _rwkv_layer(h, gn, mu, wr, wk, wv, wo, w0, w1, w2, a0, a1, a2, g1, g2, k_k, k_a, r_k, lnx_w, lnx_b,
                vres, tm):
    B, S, D = h.shape
    rows = [w0, a0, k_k, k_a]
    extra = None
    if vres is not None:
        v0, v1, v2, v_first = vres
        rows.append(v0)
        extra = (v1.astype(BF16), v2.astype(BF16), v_first)
    vec = jnp.stack(rows + [jnp.zeros_like(w0)] * (SUBLANES - len(rows)))
    mu8 = jnp.concatenate([mu, jnp.zeros((SUBLANES - mu.shape[0], D), F32)])
    ws = [w.astype(BF16) for w in (wr, wk, wv, w1, w2, a1, a2, g1, g2)]
    r, lw, k, v, a, b, g = _rwkv_proj(h, gn, mu8, vec, ws, extra, tm)
    vec2 = jnp.stack([r_k.reshape(D), lnx_w, lnx_b] + [jnp.zeros_like(w0)] * (SUBLANES - 3))
    y = _wkv(r, lw, k, v, a, b, g, vec2)
    out = _linear_res(y.reshape(B * S, D), wo.astype(BF16), jnp.zeros((D,), F32), h.reshape(B * S, D), tm)
    return out.reshape(B, S, D), v


def _glu_kernel(h_ref, g_ref, w_ref, b_ref, z_ref):
    C = z_ref.shape[-1]
    u = _rms(h_ref[...], g_ref[...]).astype(BF16)
    z = _dot(u, w_ref[...]) + b_ref[...]
    z_ref[...] = z[:, :C] * _sigmoid(z[:, C:])


def _glu(h2, g, w1, b1, tm):
    T, D = h2.shape
    N = w1.shape[1]
    return pl.pallas_call(
        _glu_kernel,
        grid=(T // tm,),
        in_specs=[pl.BlockSpec((tm, D), lambda i: (i, 0)), _full((1, D)), _full((D, N)), _full((1, N))],
        out_specs=pl.BlockSpec((tm, N // 2), lambda i: (i, 0)),
        out_shape=jax.ShapeDtypeStruct((T, N // 2), F32),
        compiler_params=_cparams("parallel"),
        name="conv_glu",
    )(h2, g.reshape(1, D), w1, b1.reshape(1, N))


_CONV_HALO = 32


def _dwconv_kernel(tm, z_ref, zp_ref, h_ref, dw_ref, vec_ref, w2_ref, o_ref, zbuf):
    i = pl.program_id(1)
    zbuf[0:_CONV_HALO, :] = jnp.where(i > 0, zp_ref[0], 0.0)
    zbuf[_CONV_HALO:, :] = z_ref[0]
    dw = dw_ref[...]
    vec = vec_ref[...]
    bdw, ln_w, ln_b, b2 = vec[0:1], vec[1:2], vec[2:3], vec[3:4]
    off = _CONV_HALO - (CONV_WIDTH - 1)
    acc = zbuf[pl.ds(off, tm), :] * dw[0:1, :]
    for j in range(1, CONV_WIDTH):
        acc = acc + zbuf[pl.ds(off + j, tm), :] * dw[j:j + 1, :]
    acc = acc + bdw
    mean = jnp.mean(acc, axis=-1, keepdims=True)
    d = acc - mean
    var = jnp.mean(d * d, axis=-1, keepdims=True)
    y = d * lax.rsqrt(var + NORM_EPS) * ln_w + ln_b
    y = y * _sigmoid(y)
    o_ref[0] = h_ref[0] + b2 + _dot(y.astype(BF16), w2_ref[...])


def _dwconv(z, h, dw, vec, w2, tm):
    B, S, C = z.shape
    D = w2.shape[1]
    tok = pl.BlockSpec((1, tm, C), lambda b, i: (b, i, 0))
    prev = pl.BlockSpec((1, _CONV_HALO, C), lambda b, i: (b, jnp.maximum(i * (tm // _CONV_HALO) - 1, 0), 0))
    return pl.pallas_call(
        functools.partial(_dwconv_kernel, tm),
        grid=(B, S // tm),
        in_specs=[tok, prev, pl.BlockSpec((1, tm, D), lambda b, i: (b, i, 0)),
                  _full(dw.shape), _full(vec.shape), _full(w2.shape)],
        out_specs=pl.BlockSpec((1, tm, D), lambda b, i: (b, i, 0)),
        out_shape=jax.ShapeDtypeStruct((B, S, D), F32),
        scratch_shapes=[pltpu.VMEM((_CONV_HALO + tm, C), F32)],
        compiler_params=_cparams("parallel", "parallel"),
        name="conv_dw",
    )(z, z, h, dw, vec, w2)


def _conv_layer(h, gn, w1, b1, dw, bdw, ln_w, ln_b, w2, b2, tm):
    B, S, D = h.shape
    z = _glu(h.reshape(B * S, D), gn, w1.astype(BF16), b1, tm).reshape(B, S, -1)
    dw32 = jnp.concatenate([dw, jnp.zeros((32 - CONV_WIDTH, dw.shape[1]), F32)])
    vec = jnp.stack([bdw, ln_w, ln_b, b2] + [jnp.zeros_like(b2)] * (SUBLANES - 4))
    return _dwconv(z, h, dw32, vec, w2.astype(BF16), tm)


def _rope_table_kernel(pos_ref, frq_ref, cos_ref, sin_ref):
    ang = pos_ref[...] * frq_ref[...]
    cos_ref[...] = jnp.cos(ang)
    sin_ref[...] = jnp.sin(ang)


def _rope_tables(positions):
    B, S = positions.shape
    half = QK_ROPE // 2
    per_row = LANES // half
    rows = B * S // per_row
    inv_freq = ROPE_THETA ** (-jnp.arange(0, QK_ROPE, 2, dtype=F32) / QK_ROPE)
    pos = jnp.repeat(positions.astype(F32).reshape(rows, per_row), half, axis=1)
    frq = jnp.tile(inv_freq, per_row).reshape(1, LANES)
    tr = min(rows, 512)
    cos, sin = pl.pallas_call(
        _rope_table_kernel,
        grid=(rows // tr,),
        in_specs=[pl.BlockSpec((tr, LANES), lambda i: (i, 0)), _full((1, LANES))],
        out_specs=[pl.BlockSpec((tr, LANES), lambda i: (i, 0))] * 2,
        out_shape=[jax.ShapeDtypeStruct((rows, LANES), F32)] * 2,
        compiler_params=_cparams("parallel"),
        name="rope_table",
    )(pos, frq)
    cos = cos.reshape(B * S, half)
    sin = sin.reshape(B * S, half)
    cos_t = jnp.tile(cos, (1, LANES // half))
    sin_t = jnp.tile(jnp.concatenate([-sin, sin], axis=1), (1, LANES // QK_ROPE))
    return cos_t, sin_t


def _mla_proj_kernel(scale, h_ref, gn_ref, wd_ref, gq_ref, gkv_ref, wqn_ref, wqr_ref, wqs_ref, wkn_ref, wv_ref,
                     cos_ref, sin_ref, qn_out, qr_out, kn_out, kr_out, v_out):
    u = _rms(h_ref[...], gn_ref[...]).astype(BF16)
    c = _dot(u, wd_ref[...])
    cq = _rms(c[:, :Q_LORA], gq_ref[...]).astype(BF16)
    ckv = _rms(c[:, Q_LORA:Q_LORA + KV_LORA], gkv_ref[...]).astype(BF16)
    o = Q_LORA + KV_LORA
    cos = cos_ref[...]
    sin = sin_ref[...]
    kr_out[...] = (c[:, o:o + LANES] * cos + c[:, o + LANES:o + 2 * LANES] * sin).astype(BF16)
    reps = qr_out.shape[-1] // LANES
    cos_w = jnp.concatenate([cos] * reps, axis=-1)
    sin_w = jnp.concatenate([sin] * reps, axis=-1)
    qn_out[...] = (_dot(cq, wqn_ref[...]) * scale).astype(BF16)
    qr = _dot(cq, wqr_ref[...]) * cos_w + _dot(cq, wqs_ref[...]) * sin_w
    qr_out[...] = (qr * scale).astype(BF16)
    kn_out[...] = _dot(ckv, wkn_ref[...]).astype(BF16)
    v_out[...] = _dot(ckv, wv_ref[...]).astype(BF16)


def _swap_halves(w):
    half = QK_ROPE // 2
    return jnp.concatenate([w[..., half:], w[..., :half]], axis=-1)


def _mla_proj(h2, gn, wd, gq, wuq, gkv, wukv, cos_t, sin_t, tm):
    T, D = h2.shape
    H = MLA_HEADS
    npair = H // 2
    HD = H * QK_NOPE
    kr_w = wd[:, Q_LORA + KV_LORA:]
    zpad = jnp.zeros((D, LANES - 2 * QK_ROPE), F32)
    wd2 = jnp.concatenate([wd[:, :Q_LORA + KV_LORA], kr_w, kr_w, zpad,
                           _swap_halves(kr_w), _swap_halves(kr_w), zpad], axis=1).astype(BF16)
    wq = wuq.reshape(Q_LORA, H, QK_NOPE + QK_ROPE)
    wqn = wq[:, :, :QK_NOPE].reshape(Q_LORA, HD).astype(BF16)
    wq_rope = wq[:, :, QK_NOPE:]

    def pair_pad(w):
        w = w.reshape(Q_LORA, npair, 2 * QK_ROPE)
        w = jnp.concatenate([w, jnp.zeros((Q_LORA, npair, LANES - 2 * QK_ROPE), F32)], axis=-1)
        return w.reshape(Q_LORA, npair * LANES).astype(BF16)

    wqr = pair_pad(wq_rope)
    wqs = pair_pad(_swap_halves(wq_rope))
    wkv = wukv.reshape(KV_LORA, H, QK_NOPE + V_DIM)
    wkn = wkv[:, :, :QK_NOPE].reshape(KV_LORA, HD).astype(BF16)
    wv = wkv[:, :, QK_NOPE:].reshape(KV_LORA, H * V_DIM).astype(BF16)
    scale = 1.0 / math.sqrt(QK_NOPE + QK_ROPE)
    ws = [wd2, gq.reshape(1, -1), gkv.reshape(1, -1), wqn, wqr, wqs, wkn, wv]
    tokw = lambda n: pl.BlockSpec((tm, n), lambda i: (i, 0))
    return pl.pallas_call(
        functools.partial(_mla_proj_kernel, scale),
        grid=(T // tm,),
        in_specs=[tokw(D), _full((1, D))] + [_full(w.shape) for w in ws] + [tokw(LANES), tokw(LANES)],
        out_specs=[tokw(HD), tokw(npair * LANES), tokw(HD), tokw(LANES), tokw(H * V_DIM)],
        out_shape=[jax.ShapeDtypeStruct((T, HD), BF16), jax.ShapeDtypeStruct((T, npair * LANES), BF16),
                   jax.ShapeDtypeStruct((T, HD), BF16), jax.ShapeDtypeStruct((T, LANES), BF16),
                   jax.ShapeDtypeStruct((T, H * V_DIM), BF16)],
        compiler_params=_cparams("parallel"),
        name="mla_proj",
    )(h2, gn.reshape(1, D), *ws, cos_t, sin_t)


def _attn_kernel(tq, qn_ref, qr_ref, kn_ref, kr_ref, v_ref, o_ref):
    S = qn_ref.shape[1]
    lane = lax.broadcasted_iota(jnp.int32, (1, 2 * LANES), 1)
    head0 = (lane < QK_NOPE) | ((lane >= LANES) & (lane < LANES + QK_ROPE))
    head1 = ((lane >= QK_NOPE) & (lane < LANES)) | ((lane >= LANES + QK_ROPE) & (lane < LANES + 2 * QK_ROPE))
    row = lax.broadcasted_iota(jnp.int32, (2 * tq, tq), 0) % tq
    col = lax.broadcasted_iota(jnp.int32, (2 * tq, tq), 1)
    causal = col <= row
    out_lane = lax.broadcasted_iota(jnp.int32, (1, LANES), 1)
    for i in range(S // tq):
        q0, n = i * tq, (i + 1) * tq
        qc = jnp.concatenate([qn_ref[0, q0:n, :], qr_ref[0, q0:n, :]], axis=-1)
        zero = jnp.zeros_like(qc)
        qs = jnp.concatenate([jnp.where(head0, qc, zero), jnp.where(head1, qc, zero)], axis=0)
        kc = jnp.concatenate([kn_ref[0, :n, :], kr_ref[0, :n, :]], axis=-1)
        s = lax.dot_general(qs, kc, _NT, preferred_element_type=F32)
        s_diag = jnp.where(causal, s[:, q0:], -1e30)
        s = jnp.concatenate([s[:, :q0], s_diag], axis=-1) if i else s_diag
        m = jnp.max(s, axis=-1, keepdims=True)
        p = jnp.exp(s - m)
        l = jnp.sum(p, axis=-1, keepdims=True)
        o = _dot(p.astype(BF16), v_ref[0, :n, :]) / l
        o_ref[0, q0:n, :] = jnp.where(out_lane < V_DIM, o[:tq], o[tq:]).astype(o_ref.dtype)


def _attention(qn, qr, kn, kr, v, tq):
    B, S, HD = qn.shape
    npair = HD // LANES
    spec = pl.BlockSpec((1, S, LANES), lambda b, p: (b, 0, p))
    return pl.pallas_call(
        functools.partial(_attn_kernel, tq),
        grid=(B, npair),
        in_specs=[spec, spec, spec, pl.BlockSpec((1, S, LANES), lambda b, p: (b, 0, 0)), spec],
        out_specs=spec,
        out_shape=jax.ShapeDtypeStruct((B, S, HD), BF16),
        compiler_params=_cparams("parallel", "parallel"),
        name="mla_attn",
    )(qn, qr, kn, kr, v)


def _mla_layer(h, positions, gn, wd, gq, wuq, gkv, wukv, wo, tm, tq):
    B, S, D = h.shape
    cos_t, sin_t = _rope_tables(positions)
    qn, qr, kn, kr, v = _mla_proj(h.reshape(B * S, D), gn, wd, gq, wuq, gkv, wukv, cos_t, sin_t, tm)
    sh = lambda t: t.reshape(B, S, t.shape[-1])
    o = _attention(sh(qn), sh(qr), sh(kn), sh(kr), sh(v), tq)
    out = _linear_res(o.reshape(B * S, -1), wo.astype(BF16), jnp.zeros((D,), F32), h.reshape(B * S, D), tm)
    return out.reshape(B, S, D)


def _tile(n, pref):
    t = min(n, pref)
    while n % t:
        t //= 2
    return t


def kernel(x, positions, norm_mix, norm_ffn, norm_final, rw_mu, rw_wr, rw_wk, rw_wv, rw_wo, rw_w0, rw_w1, rw_w2, rw_a0, rw_a1, rw_a2, rw_g1, rw_g2, rw_kk, rw_ka, rw_rk, rw_lnx_w, rw_lnx_b, rw_v0, rw_v1, rw_v2, cv_w1, cv_b1, cv_dw, cv_bdw, cv_ln_w, cv_ln_b, cv_w2, cv_b2, ml_wd, ml_gq, ml_wuq, ml_gkv, ml_wukv, ml_wo, ff_w1, ff_w2):
    B, S, D = x.shape
    depth = norm_mix.shape[0]
    tm = _tile(S, 512)
    tq = _tile(S, 256)
    tf = _tile(ff_w1.shape[-1], 1024)
    h = x
    v_first = None
    ia = ib = ic = 0
    for i in range(depth):
        kind = i % 3
        if kind == 0:
            vres = None if ia == 0 else (rw_v0[ia - 1], rw_v1[ia - 1], rw_v2[ia - 1], v_first)
            h, v = _rwkv_layer(h, norm_mix[i], rw_mu[ia], rw_wr[ia], rw_wk[ia], rw_wv[ia], rw_wo[ia],
                               rw_w0[ia], rw_w1[ia], rw_w2[ia], rw_a0[ia], rw_a1[ia], rw_a2[ia],
                               rw_g1[ia], rw_g2[ia], rw_kk[ia], rw_ka[ia], rw_rk[ia],
                               rw_lnx_w[ia], rw_lnx_b[ia], vres, tm)
            if ia == 0:
                v_first = v
            ia += 1
        elif kind == 1:
            h = _conv_layer(h, norm_mix[i], cv_w1[ib], cv_b1[ib], cv_dw[ib], cv_bdw[ib],
                            cv_ln_w[ib], cv_ln_b[ib], cv_w2[ib], cv_b2[ib], tm)
            ib += 1
        else:
            h = _mla_layer(h, positions, norm_mix[i], ml_wd[ic], ml_gq[ic], ml_wuq[ic],
                           ml_gkv[ic], ml_wukv[ic], ml_wo[ic], tm, tq)
            ic += 1
        h = _mlp(h.reshape(B * S, D), norm_ffn[i], ff_w1[i].astype(BF16), ff_w2[i].astype(BF16),
                 norm_final, i == depth - 1, tm, tf).reshape(B, S, D)
    return h
```

```python
import functools
import math

import jax
import jax.numpy as jnp
from jax import lax
from jax.experimental import pallas as pl
from jax.experimental.pallas import tpu as pltpu

F32 = jnp.float32
BF16 = jnp.bfloat16

NORM_EPS = 1e-5
GN_EPS = 64e-5
RWKV_HEAD = 64
CONV_WIDTH = 31
MLA_HEADS = 16
QK_NOPE = 64
QK_ROPE = 32
V_DIM = 64
Q_LORA = 384
KV_LORA = 256
ROPE_THETA = 10000.0

LANES = 128
SUBLANES = 8
MXU_COLS = 256
WKV_CHUNK = 64
WKV_ROWS = 4
VMEM_LIMIT_BYTES = 56 * 1024 * 1024

_NT = (((1,), (1,)), ((), ()))
_TN = (((0,), (0,)), ((), ()))


def _cparams(*sem):
    return pltpu.CompilerParams(dimension_semantics=sem, vmem_limit_bytes=VMEM_LIMIT_BYTES)


def _dot(a, b):
    return jnp.dot(a, b, preferred_element_type=F32)


def _rms(x, g):
    ms = jnp.mean(x * x, axis=-1, keepdims=True)
    return x * lax.rsqrt(ms + NORM_EPS) * g


def _sigmoid(x):
    return 1.0 / (1.0 + jnp.exp(-x))


def _softplus(x):
    return jnp.maximum(x, 0.0) + jnp.log(1.0 + jnp.exp(-jnp.abs(x)))


def _head_sum(x, bd):
    outs = []
    for c in range(x.shape[-1] // MXU_COLS):
        xc = x[:, c * MXU_COLS:(c + 1) * MXU_COLS]
        hi = xc.astype(BF16)
        lo = (xc - hi.astype(F32)).astype(BF16)
        outs.append(_dot(hi, bd) + _dot(lo, bd))
    return jnp.concatenate(outs, axis=-1)


def _full(shape):
    n = len(shape)
    return pl.BlockSpec(shape, lambda *_: (0,) * n)


def _mlp_kernel(final_norm, tf, h_ref, y_ref, wo_ref, bo_ref, g_ref, w1_ref, w2_ref, gf_ref, o_ref, a_scr):
    h = h_ref[...] + bo_ref[...] + _dot(y_ref[...], wo_ref[...])
    u = _rms(h, g_ref[...]).astype(BF16)
    for c in range(a_scr.shape[1] // tf):
        cols = slice(c * tf, (c + 1) * tf)
        a = _dot(u, w1_ref[:, cols])
        a_scr[:, cols] = jnp.square(jnp.maximum(a, 0.0)).astype(BF16)
    out = h + _dot(a_scr[...], w2_ref[...])
    if final_norm:
        out = _rms(out, gf_ref[...])
    o_ref[...] = out


def _resident(shape):
    n = len(shape)
    return pl.BlockSpec(shape, lambda *_: (0,) * n, pipeline_mode=pl.Buffered(1))


def _mlp(h2, y2, wo, bo, g, w1, w2, gf, final_norm, tm, tf):
    T, D = h2.shape
    K = y2.shape[1]
    F = w1.shape[1]
    return pl.pallas_call(
        functools.partial(_mlp_kernel, final_norm, tf),
        grid=(T // tm,),
        in_specs=[
            pl.BlockSpec((tm, D), lambda i: (i, 0)),
            pl.BlockSpec((tm, K), lambda i: (i, 0)),
            _resident((K, D)),
            _full((1, D)),
            _full((1, D)),
            _resident((D, F)),
            _resident((F, D)),
            _full((1, D)),
        ],
        out_specs=pl.BlockSpec((tm, D), lambda i: (i, 0)),
        out_shape=jax.ShapeDtypeStruct((T, D), F32),
        scratch_shapes=[pltpu.VMEM((tm, F), BF16)],
        compiler_params=_cparams("parallel"),
        name="mlp",
    )(h2, y2, wo, bo.reshape(1, D), g.reshape(1, D), w1, w2, gf.reshape(1, D))


def _rwkv_proj_kernel(has_vres, tm, *refs):
    if has_vres:
        (h_ref, hp_ref, gn_ref, mu_ref, vec_ref, wr_ref, wk_ref, wv_ref, w1_ref, w2_ref, a1_ref, a2_ref,
         g1_ref, g2_ref, bd_ref, v1_ref, v2_ref, vf_ref,
         r_out, lw_out, k_out, v_out, a_out, b_out, g_out) = refs
    else:
        (h_ref, hp_ref, gn_ref, mu_ref, vec_ref, wr_ref, wk_ref, wv_ref, w1_ref, w2_ref, a1_ref, a2_ref,
         g1_ref, g2_ref, bd_ref,
         r_out, lw_out, k_out, v_out, a_out, b_out, g_out) = refs
    i = pl.program_id(1)
    gn = gn_ref[...]
    u = _rms(h_ref[0], gn)
    prev = _rms(hp_ref[0], gn)[SUBLANES - 1:SUBLANES, :]
    prev = jnp.where(i > 0, prev, 0.0)
    row = lax.broadcasted_iota(jnp.int32, (tm, 1), 0)
    shifted = jnp.where(row == 0, prev, pltpu.roll(u, 1, axis=0))
    xx = shifted - u
    mu = mu_ref[...]
    vec = vec_ref[...]
    w0, a0, k_k, k_a = vec[0:1], vec[1:2], vec[2:3], vec[3:4]

    def mix(j):
        return (u + xx * mu[j:j + 1, :]).astype(BF16)

    xr, xw, xk, xv, xa, xg = [mix(j) for j in range(6)]
    r = _dot(xr, wr_ref[...])
    k = _dot(xk, wk_ref[...])
    v = _dot(xv, wv_ref[...])
    wl = w0 + _dot(jnp.tanh(_dot(xw, w1_ref[...])).astype(BF16), w2_ref[...])
    w = -_softplus(-wl) - 0.5
    a = _sigmoid(a0 + _dot(_dot(xa, a1_ref[...]).astype(BF16), a2_ref[...]))
    g = _dot(_sigmoid(_dot(xg, g1_ref[...])).astype(BF16), g2_ref[...])
    if has_vres:
        v0 = vec[4:5]
        gate = _sigmoid(v0 + _dot(_dot(xv, v1_ref[...]).astype(BF16), v2_ref[...]))
        v = v + (vf_ref[0] - v) * gate
    kk = k * k_k
    ss = _head_sum(kk * kk, bd_ref[...])
    kk = kk * lax.rsqrt(jnp.maximum(ss, 1e-24))
    r_out[0] = r
    lw_out[0] = -jnp.exp(w)
    k_out[0] = k * (1.0 + (a - 1.0) * k_a)
    v_out[0] = v
    a_out[0] = -kk
    b_out[0] = kk * a
    g_out[0] = g


def _block_diag_ones(n, blk):
    idx = jnp.arange(n) // blk
    return (idx[:, None] == idx[None, :]).astype(BF16)


def _rwkv_proj(h, gn, mu, vec, ws, vres, tm):
    B, S, D = h.shape
    has_vres = vres is not None
    tok = pl.BlockSpec((1, tm, D), lambda b, i: (b, i, 0))
    prev = pl.BlockSpec((1, SUBLANES, D), lambda b, i: (b, jnp.maximum(i * (tm // SUBLANES) - 1, 0), 0))
    args = [h, h, gn.reshape(1, D), mu, vec] + list(ws) + [_block_diag_ones(MXU_COLS, RWKV_HEAD)]
    specs = [tok, prev, _full((1, D)), _full(mu.shape), _full(vec.shape)] + [_full(w.shape) for w in ws]
    specs.append(_full((MXU_COLS, MXU_COLS)))
    if has_vres:
        v1, v2, v_first = vres
        args += [v1, v2, v_first]
        specs += [_full(v1.shape), _full(v2.shape), tok]
    out = jax.ShapeDtypeStruct((B, S, D), F32)
    return pl.pallas_call(
        functools.partial(_rwkv_proj_kernel, has_vres, tm),
        grid=(B, S // tm),
        in_specs=specs,
        out_specs=[tok] * 7,
        out_shape=[out] * 7,
        compiler_params=_cparams("parallel", "parallel"),
        name="rwkv_proj",
    )(*args)


def _wkv_kernel(r_ref, lw_ref, k_ref, v_ref, a_ref, b_ref, g_ref, vec_ref, tri_ref, bd_ref,
                y_ref, h_scr, y_scr):
    L = WKV_CHUNK
    P = 2 * L
    nb, _, D = r_ref.shape
    npair = D // LANES

    @pl.when(pl.program_id(1) == 0)
    def _():
        h_scr[...] = jnp.zeros_like(h_scr)

    lane = lax.broadcasted_iota(jnp.int32, (1, LANES), 1)
    first = lane < RWKV_HEAD
    row = lax.broadcasted_iota(jnp.int32, (P, P), 0)
    col = lax.broadcasted_iota(jnp.int32, (P, P), 1)
    lower = row > col
    lower_incl = row >= col
    eye = (row == col).astype(F32)

    def stack(x):
        return jnp.concatenate([jnp.where(first, x, 0.0), jnp.where(first, 0.0, x)], axis=0).astype(BF16)

    def stack2(x, y, sl):
        return jnp.concatenate([stack(x[:, sl]), stack(y[:, sl])], axis=0)

    ar, bk, blkl, v_s, h0, w_end = [], [], [], [], [], []
    for bi in range(nb):
        lw = lw_ref[bi]
        hi = lw.astype(BF16)
        rest = lw - hi.astype(F32)
        mid = rest.astype(BF16)
        low = (rest - mid.astype(F32)).astype(BF16)
        tri = tri_ref[...]
        cum = _dot(tri, hi) + _dot(tri, mid) + _dot(tri, low)
        e_pos = jnp.exp(cum)
        e_neg = jnp.exp(-cum)
        rt = r_ref[bi] * e_pos
        at = a_ref[bi] * jnp.exp(cum - lw)
        kt = k_ref[bi] * e_neg
        bt = b_ref[bi] * e_neg
        we = e_pos[L - 1:L, :]
        kl = kt * we
        bl = bt * we
        v = v_ref[bi]
        for p in range(npair):
            sl = slice(p * LANES, (p + 1) * LANES)
            ar.append(stack2(at, rt, sl))
            bk.append(stack2(bt, kt, sl))
            blkl.append(stack2(bl, kl, sl))
            v_s.append(stack(v[:, sl]))
            h0.append(h_scr[bi * npair + p])
            w_end.append(we[:, sl])

    sc = [lax.dot_general(x, y, _NT, preferred_element_type=F32) for x, y in zip(ar, bk)]
    m_abt = [jnp.transpose(jnp.where(lower, s[:P, :P], 0.0)) for s in sc]
    m_rb = [jnp.where(lower_incl, s[P:, :P], 0.0).astype(BF16) for s in sc]
    m_k = [jnp.concatenate([jnp.where(lower, s[:P, P:], 0.0), jnp.where(lower_incl, s[P:, P:], 0.0)],
                           axis=0).astype(BF16) for s in sc]
    xy = [_dot(jnp.concatenate([x, m], axis=1), jnp.concatenate([h.astype(BF16), vs], axis=0))
          for x, m, h, vs in zip(ar, m_k, h0, v_s)]
    t_t = [eye + m for m in m_abt]
    a_cur = [m.astype(BF16) for m in m_abt]
    a_cur = [_dot(m, m).astype(BF16) for m in a_cur]
    n = 4
    while n < L:
        res = [_dot(m, jnp.concatenate([m, t.astype(BF16)], axis=1)) for m, t in zip(a_cur, t_t)]
        t_t = [t + x[:, P:] for t, x in zip(t_t, res)]
        a_cur = [x[:, :P].astype(BF16) for x in res]
        n *= 2
    t_t = [t + _dot(m, t.astype(BF16)) for t, m in zip(t_t, a_cur)]
    u_s = [lax.dot_general(t.astype(BF16), x[:P].astype(BF16), _TN,
                           preferred_element_type=F32).astype(BF16) for t, x in zip(t_t, xy)]
    ys = [x[P:] + _dot(m, u) for x, m, u in zip(xy, m_rb, u_s)]
    dh = [lax.dot_general(w, jnp.concatenate([u, vs], axis=0), _TN, preferred_element_type=F32)
          for w, u, vs in zip(blkl, u_s, v_s)]
    for j in range(nb * npair):
        bi, p = divmod(j, npair)
        y_scr[bi, :, p * LANES:(p + 1) * LANES] = ys[j][:L] + ys[j][L:]
        decay = jnp.transpose(jnp.broadcast_to(w_end[j], (LANES, LANES)))
        h_scr[j] = h0[j] * decay + dh[j]

    vec = vec_ref[...]
    r_k, ln_w, ln_b = vec[0:1], vec[1:2], vec[2:3]
    bd = bd_ref[...]
    inv_n = 1.0 / RWKV_HEAD
    for bi in range(nb):
        y = y_scr[bi]
        d = y - _head_sum(y, bd) * inv_n
        var = _head_sum(d * d, bd) * inv_n
        yn = d * lax.rsqrt(var + GN_EPS) * ln_w + ln_b
        bonus = _head_sum(r_ref[bi] * k_ref[bi] * r_k, bd) * v_ref[bi]
        y_ref[bi] = ((yn + bonus) * g_ref[bi]).astype(y_ref.dtype)


def _wkv(r, lw, k, v, a, b, g, vec, nb):
    B, S, D = r.shape
    L = WKV_CHUNK
    tok = pl.BlockSpec((nb, L, D), lambda bi, c: (bi, c, 0))
    tri = (jnp.arange(L)[:, None] >= jnp.arange(L)[None, :]).astype(BF16)
    return pl.pallas_call(
        _wkv_kernel,
        grid=(B // nb, S // L),
        in_specs=[tok] * 7 + [_full(vec.shape), _full((L, L)), _full((MXU_COLS, MXU_COLS))],
        out_specs=tok,
        out_shape=jax.ShapeDtypeStruct((B, S, D), BF16),
        scratch_shapes=[pltpu.VMEM((nb * D // LANES, LANES, LANES), F32), pltpu.VMEM((nb, L, D), F32)],
        compiler_params=_cparams("parallel", "arbitrary"),
        name="wkv",
    )(r, lw, k, v, a, b, g, vec, tri, _block_diag_ones(MXU_COLS, RWKV_HEAD))


def _rwkv_layer(h, gn, mu, wr, wk, wv, w0, w1, w2, a0, a1, a2, g1, g2, k_k, k_a, r_k, lnx_w, lnx_b,
                vres, tm):
    B, S, D = h.shape
    rows = [w0, a0, k_k, k_a]
    extra = None
    if vres is not None:
        v0, v1, v2, v_first = vres
        rows.append(v0)
        extra = (v1.astype(BF16), v2.astype(BF16), v_first)
    vec = jnp.stack(rows + [jnp.zeros_like(w0)] * (SUBLANES - len(rows)))
    mu8 = jnp.concatenate([mu, jnp.zeros((SUBLANES - mu.shape[0], D), F32)])
    ws = [w.astype(BF16) for w in (wr, wk, wv, w1, w2, a1, a2, g1, g2)]
    r, lw, k, v, a, b, g = _rwkv_proj(h, gn, mu8, vec, ws, extra, tm)
    vec2 = jnp.stack([r_k.reshape(D), lnx_w, lnx_b] + [jnp.zeros_like(w0)] * (SUBLANES - 3))
    return _wkv(r, lw, k, v, a, b, g, vec2, _tile(B, WKV_ROWS)), v


def _glu_kernel(h_ref, g_ref, w_ref, b_ref, z_ref):
    C = z_ref.shape[-1]
    u = _rms(h_ref[...], g_ref[...]).astype(BF16)
    z = _dot(u, w_ref[...]) + b_ref[...]
    z_ref[...] = z[:, :C] * _sigmoid(z[:, C:])


def _glu(h2, g, w1, b1, tm):
    T, D = h2.shape
    N = w1.shape[1]
    return pl.pallas_call(
        _glu_kernel,
        grid=(T // tm,),
        in_specs=[pl.BlockSpec((tm, D), lambda i: (i, 0)), _full((1, D)), _full((D, N)), _full((1, N))],
        out_specs=pl.BlockSpec((tm, N // 2), lambda i: (i, 0)),
        out_shape=jax.ShapeDtypeStruct((T, N // 2), F32),
        compiler_params=_cparams("parallel"),
        name="conv_glu",
    )(h2, g.reshape(1, D), w1, b1.reshape(1, N))


_CONV_HALO = 32


_CONV_ROWS = 64
_CONV_COLS = 128


def _dwconv_kernel(tm, z_ref, zp_ref, dw_ref, vec_ref, o_ref, zbuf):
    i = pl.program_id(1)
    C = z_ref.shape[-1]
    zbuf[0:_CONV_HALO, :] = jnp.where(i > 0, zp_ref[0], 0.0)
    zbuf[_CONV_HALO:, :] = z_ref[0]
    vec = vec_ref[...]
    bdw, ln_w, ln_b = vec[0:1], vec[1:2], vec[2:3]
    off = _CONV_HALO - (CONV_WIDTH - 1)
    span = _CONV_ROWS + _CONV_HALO
    for r0 in range(0, tm, _CONV_ROWS):
        parts = []
        for c0 in range(0, C, _CONV_COLS):
            cols = slice(c0, c0 + _CONV_COLS)
            acc = None
            win = zbuf[r0:r0 + span, cols]
            for res in range(SUBLANES):
                shifted = pltpu.roll(win, span - res, axis=0) if res else win
                for o in range(off + (res - off) % SUBLANES, off + CONV_WIDTH, SUBLANES):
                    q = o - res
                    term = shifted[q:q + _CONV_ROWS, :] * dw_ref[o - off:o - off + 1, cols]
                    acc = term if acc is None else acc + term
            parts.append(acc)
        acc = jnp.concatenate(parts, axis=-1) + bdw
        mean = jnp.mean(acc, axis=-1, keepdims=True)
        d = acc - mean
        var = jnp.mean(d * d, axis=-1, keepdims=True)
        y = d * lax.rsqrt(var + NORM_EPS) * ln_w + ln_b
        o_ref[0, r0:r0 + _CONV_ROWS, :] = (y * _sigmoid(y)).astype(o_ref.dtype)


def _dwconv(z, dw, vec, tm):
    B, S, C = z.shape
    tok = pl.BlockSpec((1, tm, C), lambda b, i: (b, i, 0))
    prev = pl.BlockSpec((1, _CONV_HALO, C), lambda b, i: (b, jnp.maximum(i * (tm // _CONV_HALO) - 1, 0), 0))
    return pl.pallas_call(
        functools.partial(_dwconv_kernel, tm),
        grid=(B, S // tm),
        in_specs=[tok, prev, _full(dw.shape), _full(vec.shape)],
        out_specs=tok,
        out_shape=jax.ShapeDtypeStruct((B, S, C), BF16),
        scratch_shapes=[pltpu.VMEM((_CONV_HALO + tm, C), F32)],
        compiler_params=_cparams("parallel", "parallel"),
        name="conv_dw",
    )(z, z, dw, vec)


def _conv_layer(h, gn, w1, b1, dw, bdw, ln_w, ln_b, tm):
    B, S, D = h.shape
    z = _glu(h.reshape(B * S, D), gn, w1.astype(BF16), b1, tm).reshape(B, S, -1)
    dw32 = jnp.concatenate([dw, jnp.zeros((32 - CONV_WIDTH, dw.shape[1]), F32)])
    vec = jnp.stack([bdw, ln_w, ln_b] + [jnp.zeros_like(bdw)] * (SUBLANES - 3))
    return _dwconv(z, dw32, vec, tm)


def _rope_table_kernel(pos_ref, frq_ref, cos_ref, sin_ref):
    ang = pos_ref[...] * frq_ref[...]
    cos_ref[...] = jnp.cos(ang)
    sin_ref[...] = jnp.sin(ang)


def _rope_tables(positions):
    B, S = positions.shape
    half = QK_ROPE // 2
    per_row = LANES // half
    rows = B * S // per_row
    inv_freq = ROPE_THETA ** (-jnp.arange(0, QK_ROPE, 2, dtype=F32) / QK_ROPE)
    pos = jnp.repeat(positions.astype(F32).reshape(rows, per_row), half, axis=1)
    frq = jnp.tile(inv_freq, per_row).reshape(1, LANES)
    tr = min(rows, 512)
    cos, sin = pl.pallas_call(
        _rope_table_kernel,
        grid=(rows // tr,),
        in_specs=[pl.BlockSpec((tr, LANES), lambda i: (i, 0)), _full((1, LANES))],
        out_specs=[pl.BlockSpec((tr, LANES), lambda i: (i, 0))] * 2,
        out_shape=[jax.ShapeDtypeStruct((rows, LANES), F32)] * 2,
        compiler_params=_cparams("parallel"),
        name="rope_table",
    )(pos, frq)
    cos = cos.reshape(B * S, half)
    sin = sin.reshape(B * S, half)
    cos_t = jnp.tile(cos, (1, LANES // half))
    sin_t = jnp.tile(jnp.concatenate([-sin, sin], axis=1), (1, LANES // QK_ROPE))
    return cos_t, sin_t


def _mla_proj_kernel(scale, h_ref, gn_ref, wd_ref, gq_ref, gkv_ref, wqn_ref, wqr_ref, wqs_ref, wkn_ref, wv_ref,
                     cos_ref, sin_ref, qn_out, qr_out, kn_out, kr_out, v_out):
    u = _rms(h_ref[...], gn_ref[...]).astype(BF16)
    c = _dot(u, wd_ref[...])
    cq = _rms(c[:, :Q_LORA], gq_ref[...]).astype(BF16)
    ckv = _rms(c[:, Q_LORA:Q_LORA + KV_LORA], gkv_ref[...]).astype(BF16)
    o = Q_LORA + KV_LORA
    cos = cos_ref[...]
    sin = sin_ref[...]
    kr_out[...] = (c[:, o:o + LANES] * cos + c[:, o + LANES:o + 2 * LANES] * sin).astype(BF16)
    reps = qr_out.shape[-1] // LANES
    cos_w = jnp.concatenate([cos] * reps, axis=-1)
    sin_w = jnp.concatenate([sin] * reps, axis=-1)
    qn_out[...] = (_dot(cq, wqn_ref[...]) * scale).astype(BF16)
    qr = _dot(cq, wqr_ref[...]) * cos_w + _dot(cq, wqs_ref[...]) * sin_w
    qr_out[...] = (qr * scale).astype(BF16)
    kn_out[...] = _dot(ckv, wkn_ref[...]).astype(BF16)
    v_out[...] = _dot(ckv, wv_ref[...]).astype(BF16)


def _swap_halves(w):
    half = QK_ROPE // 2
    return jnp.concatenate([w[..., half:], w[..., :half]], axis=-1)


def _mla_proj(h2, gn, wd, gq, wuq, gkv, wukv, cos_t, sin_t, tm):
    T, D = h2.shape
    H = MLA_HEADS
    npair = H // 2
    HD = H * QK_NOPE
    kr_w = wd[:, Q_LORA + KV_LORA:]
    zpad = jnp.zeros((D, LANES - 2 * QK_ROPE), F32)
    wd2 = jnp.concatenate([wd[:, :Q_LORA + KV_LORA], kr_w, kr_w, zpad,
                           _swap_halves(kr_w), _swap_halves(kr_w), zpad], axis=1).astype(BF16)
    wq = wuq.reshape(Q_LORA, H, QK_NOPE + QK_ROPE)
    wqn = wq[:, :, :QK_NOPE].reshape(Q_LORA, HD).astype(BF16)
    wq_rope = wq[:, :, QK_NOPE:]

    def pair_pad(w):
        w = w.reshape(Q_LORA, npair, 2 * QK_ROPE)
        w = jnp.concatenate([w, jnp.zeros((Q_LORA, npair, LANES - 2 * QK_ROPE), F32)], axis=-1)
        return w.reshape(Q_LORA, npair * LANES).astype(BF16)

    wqr = pair_pad(wq_rope)
    wqs = pair_pad(_swap_halves(wq_rope))
    wkv = wukv.reshape(KV_LORA, H, QK_NOPE + V_DIM)
    wkn = wkv[:, :, :QK_NOPE].reshape(KV_LORA, HD).astype(BF16)
    wv = wkv[:, :, QK_NOPE:].reshape(KV_LORA, H * V_DIM).astype(BF16)
    scale = math.log2(math.e) / math.sqrt(QK_NOPE + QK_ROPE)
    ws = [wd2, gq.reshape(1, -1), gkv.reshape(1, -1), wqn, wqr, wqs, wkn, wv]
    tokw = lambda n: pl.BlockSpec((tm, n), lambda i: (i, 0))
    return pl.pallas_call(
        functools.partial(_mla_proj_kernel, scale),
        grid=(T // tm,),
        in_specs=[tokw(D), _full((1, D))] + [_full(w.shape) for w in ws] + [tokw(LANES), tokw(LANES)],
        out_specs=[tokw(HD), tokw(npair * LANES), tokw(HD), tokw(LANES), tokw(H * V_DIM)],
        out_shape=[jax.ShapeDtypeStruct((T, HD), BF16), jax.ShapeDtypeStruct((T, npair * LANES), BF16),
                   jax.ShapeDtypeStruct((T, HD), BF16), jax.ShapeDtypeStruct((T, LANES), BF16),
                   jax.ShapeDtypeStruct((T, H * V_DIM), BF16)],
        compiler_params=_cparams("parallel"),
        name="mla_proj",
    )(h2, gn.reshape(1, D), *ws, cos_t, sin_t)


def _attn_kernel(tq, qn_ref, qr_ref, kn_ref, kr_ref, v_ref, o_ref):
    S = qn_ref.shape[1]
    lane = lax.broadcasted_iota(jnp.int32, (1, 2 * LANES), 1)
    head0 = (lane < QK_NOPE) | ((lane >= LANES) & (lane < LANES + QK_ROPE))
    head1 = ((lane >= QK_NOPE) & (lane < LANES)) | ((lane >= LANES + QK_ROPE) & (lane < LANES + 2 * QK_ROPE))
    row = lax.broadcasted_iota(jnp.int32, (2 * tq, tq), 0) % tq
    col = lax.broadcasted_iota(jnp.int32, (2 * tq, tq), 1)
    causal = col <= row
    out_lane = lax.broadcasted_iota(jnp.int32, (1, LANES), 1)
    for i in range(S // tq):
        q0, n = i * tq, (i + 1) * tq
        qc = jnp.concatenate([qn_ref[0, q0:n, :], qr_ref[0, q0:n, :]], axis=-1)
        zero = jnp.zeros_like(qc)
        qs = jnp.concatenate([jnp.where(head0, qc, zero), jnp.where(head1, qc, zero)], axis=0)
        kc = jnp.concatenate([kn_ref[0, :n, :], kr_ref[0, :n, :]], axis=-1)
        s = lax.dot_general(qs, kc, _NT, preferred_element_type=F32)
        s_diag = jnp.where(causal, s[:, q0:], -1e30)
        s = jnp.concatenate([s[:, :q0], s_diag], axis=-1) if i else s_diag
        m = jnp.max(s, axis=-1, keepdims=True)
        p = jnp.exp2(s - m)
        l = jnp.sum(p, axis=-1, keepdims=True)
        o = _dot(p.astype(BF16), v_ref[0, :n, :]) / l
        o_ref[0, q0:n, :] = jnp.where(out_lane < V_DIM, o[:tq], o[tq:]).astype(o_ref.dtype)


def _attention(qn, qr, kn, kr, v, tq):
    B, S, HD = qn.shape
    npair = HD // LANES
    spec = pl.BlockSpec((1, S, LANES), lambda b, p: (b, 0, p))
    return pl.pallas_call(
        functools.partial(_attn_kernel, tq),
        grid=(B, npair),
        in_specs=[spec, spec, spec, pl.BlockSpec((1, S, LANES), lambda b, p: (b, 0, 0)), spec],
        out_specs=spec,
        out_shape=jax.ShapeDtypeStruct((B, S, HD), BF16),
        compiler_params=_cparams("parallel", "parallel"),
        name="mla_attn",
    )(qn, qr, kn, kr, v)


def _mla_layer(h, positions, gn, wd, gq, wuq, gkv, wukv, tm, tq):
    B, S, D = h.shape
    cos_t, sin_t = _rope_tables(positions)
    qn, qr, kn, kr, v = _mla_proj(h.reshape(B * S, D), gn, wd, gq, wuq, gkv, wukv, cos_t, sin_t, tm)
    sh = lambda t: t.reshape(B, S, t.shape[-1])
    return _attention(sh(qn), sh(qr), sh(kn), sh(kr), sh(v), tq)


def _tile(n, pref):
    t = min(n, pref)
    while n % t:
        t //= 2
    return t


def kernel(x, positions, norm_mix, norm_ffn, norm_final, rw_mu, rw_wr, rw_wk, rw_wv, rw_wo, rw_w0, rw_w1, rw_w2, rw_a0, rw_a1, rw_a2, rw_g1, rw_g2, rw_kk, rw_ka, rw_rk, rw_lnx_w, rw_lnx_b, rw_v0, rw_v1, rw_v2, cv_w1, cv_b1, cv_dw, cv_bdw, cv_ln_w, cv_ln_b, cv_w2, cv_b2, ml_wd, ml_gq, ml_wuq, ml_gkv, ml_wukv, ml_wo, ff_w1, ff_w2):
    B, S, D = x.shape
    depth = norm_mix.shape[0]
    tm = _tile(S, 512)
    tq = _tile(S, 256)
    tf = _tile(ff_w1.shape[-1], 1024)
    h = x
    v_first = None
    zero_bias = jnp.zeros((D,), F32)
    ia = ib = ic = 0
    for i in range(depth):
        kind = i % 3
        if kind == 0:
            vres = None if ia == 0 else (rw_v0[ia - 1], rw_v1[ia - 1], rw_v2[ia - 1], v_first)
            y, v = _rwkv_layer(h, norm_mix[i], rw_mu[ia], rw_wr[ia], rw_wk[ia], rw_wv[ia],
                               rw_w0[ia], rw_w1[ia], rw_w2[ia], rw_a0[ia], rw_a1[ia], rw_a2[ia],
                               rw_g1[ia], rw_g2[ia], rw_kk[ia], rw_ka[ia], rw_rk[ia],
                               rw_lnx_w[ia], rw_lnx_b[ia], vres, tm)
            wo, bo = rw_wo[ia], zero_bias
            if ia == 0:
                v_first = v
            ia += 1
        elif kind == 1:
            y = _conv_layer(h, norm_mix[i], cv_w1[ib], cv_b1[ib], cv_dw[ib], cv_bdw[ib],
                            cv_ln_w[ib], cv_ln_b[ib], tm)
            wo, bo = cv_w2[ib], cv_b2[ib]
            ib += 1
        else:
            y = _mla_layer(h, positions, norm_mix[i], ml_wd[ic], ml_gq[ic], ml_wuq[ic],
                           ml_gkv[ic], ml_wukv[ic], tm, tq)
            wo, bo = ml_wo[ic], zero_bias
            ic += 1
        h = _mlp(h.reshape(B * S, D), y.reshape(B * S, -1), wo.astype(BF16), bo, norm_ffn[i],
                 ff_w1[i].astype(BF16), ff_w2[i].astype(BF16), norm_final, i == depth - 1, tm, tf).reshape(B, S, D)
    return h
```

```python
import functools
import math

import jax
import jax.numpy as jnp
from jax import lax
from jax.experimental import pallas as pl
from jax.experimental.pallas import tpu as pltpu

F32 = jnp.float32
BF16 = jnp.bfloat16

NORM_EPS = 1e-5
GN_EPS = 64e-5
RWKV_HEAD = 64
CONV_WIDTH = 31
MLA_HEADS = 16
QK_NOPE = 64
QK_ROPE = 32
V_DIM = 64
Q_LORA = 384
KV_LORA = 256
ROPE_THETA = 10000.0

LANES = 128
SUBLANES = 8
MXU_COLS = 256
WKV_CHUNK = 64
WKV_ROWS = 4
RWKV_PROJ_SPLIT = 1
VMEM_LIMIT_BYTES = 56 * 1024 * 1024

_NT = (((1,), (1,)), ((), ()))
_TN = (((0,), (0,)), ((), ()))


def _cparams(*sem):
    return pltpu.CompilerParams(dimension_semantics=sem, vmem_limit_bytes=VMEM_LIMIT_BYTES)


def _dot(a, b):
    return jnp.dot(a, b, preferred_element_type=F32)


def _rms(x, g):
    ms = jnp.mean(x * x, axis=-1, keepdims=True)
    return x * lax.rsqrt(ms + NORM_EPS) * g


def _sigmoid(x):
    return 1.0 / (1.0 + jnp.exp(-x))


def _softplus(x):
    return jnp.maximum(x, 0.0) + jnp.log(1.0 + jnp.exp(-jnp.abs(x)))


def _head_sum(x, bd):
    outs = []
    for c in range(x.shape[-1] // MXU_COLS):
        xc = x[:, c * MXU_COLS:(c + 1) * MXU_COLS]
        hi = xc.astype(BF16)
        lo = (xc - hi.astype(F32)).astype(BF16)
        outs.append(_dot(hi, bd) + _dot(lo, bd))
    return jnp.concatenate(outs, axis=-1)


def _full(shape):
    n = len(shape)
    return pl.BlockSpec(shape, lambda *_: (0,) * n)


def _mlp_kernel(final_norm, tf, h_ref, y_ref, wo_ref, bo_ref, g_ref, w1_ref, w2_ref, gf_ref, o_ref, a_scr):
    h = h_ref[...] + bo_ref[...] + _dot(y_ref[...], wo_ref[...])
    u = _rms(h, g_ref[...]).astype(BF16)
    for c in range(a_scr.shape[1] // tf):
        cols = slice(c * tf, (c + 1) * tf)
        a = _dot(u, w1_ref[:, cols])
        a_scr[:, cols] = jnp.square(jnp.maximum(a, 0.0)).astype(BF16)
    out = h + _dot(a_scr[...], w2_ref[...])
    if final_norm:
        out = _rms(out, gf_ref[...])
    o_ref[...] = out


def _resident(shape):
    n = len(shape)
    return pl.BlockSpec(shape, lambda *_: (0,) * n, pipeline_mode=pl.Buffered(1))


def _mlp(h2, y2, wo, bo, g, w1_all, w2_all, layer, gf, final_norm, tm, tf):
    T, D = h2.shape
    K = y2.shape[1]
    F = w1_all.shape[2]
    one = pl.Buffered(1)
    return pl.pallas_call(
        functools.partial(_mlp_kernel, final_norm, tf),
        grid=(T // tm,),
        in_specs=[
            pl.BlockSpec((tm, D), lambda i: (i, 0)),
            pl.BlockSpec((tm, K), lambda i: (i, 0)),
            _resident((K, D)),
            _full((1, D)),
            _full((1, D)),
            pl.BlockSpec((None, D, F), lambda i: (layer, 0, 0), pipeline_mode=one),
            pl.BlockSpec((None, F, D), lambda i: (layer, 0, 0), pipeline_mode=one),
            _full((1, D)),
        ],
        out_specs=pl.BlockSpec((tm, D), lambda i: (i, 0)),
        out_shape=jax.ShapeDtypeStruct((T, D), F32),
        scratch_shapes=[pltpu.VMEM((tm, F), BF16)],
        compiler_params=_cparams("parallel"),
        name="mlp",
    )(h2, y2, wo, bo.reshape(1, D), g.reshape(1, D), w1_all, w2_all, gf.reshape(1, D))


def _rwkv_proj_kernel(has_vres, tm, *refs):
    if has_vres:
        (h_ref, hp_ref, gn_ref, mu_ref, vec_ref, wr_ref, wk_ref, wv_ref, w1_ref, w2_ref, a1_ref, a2_ref,
         g1_ref, g2_ref, bd_ref, v1_ref, v2_ref, vf_ref,
         r_out, lw_out, k_out, v_out, a_out, b_out, g_out) = refs
    else:
        (h_ref, hp_ref, gn_ref, mu_ref, vec_ref, wr_ref, wk_ref, wv_ref, w1_ref, w2_ref, a1_ref, a2_ref,
         g1_ref, g2_ref, bd_ref,
         r_out, lw_out, k_out, v_out, a_out, b_out, g_out) = refs
    i = pl.program_id(1)
    gn = gn_ref[...]
    mu = mu_ref[...]
    vec = vec_ref[...]
    w0, a0, k_k, k_a = vec[0:1], vec[1:2], vec[2:3], vec[3:4]
    prev = _rms(hp_ref[0], gn)[SUBLANES - 1:SUBLANES, :]
    prev = jnp.where(i > 0, prev, 0.0)
    nrows = tm // RWKV_PROJ_SPLIT
    row = lax.broadcasted_iota(jnp.int32, (nrows, 1), 0)
    for s in range(RWKV_PROJ_SPLIT):
        rows = slice(s * nrows, (s + 1) * nrows)
        u = _rms(h_ref[0, rows, :], gn)
        shifted = jnp.where(row == 0, prev, pltpu.roll(u, 1, axis=0))
        prev = u[nrows - 1:nrows, :]
        xx = shifted - u

        def mix(j):
            return (u + xx * mu[j:j + 1, :]).astype(BF16)

        xr, xw, xk, xv, xa, xg = [mix(j) for j in range(6)]
        lo_w = _dot(xw, w1_ref[...])
        lo_a = _dot(xa, a1_ref[...])
        lo_g = _dot(xg, g1_ref[...])
        if has_vres:
            lo_v = _dot(xv, v1_ref[...])
        r = _dot(xr, wr_ref[...])
        k = _dot(xk, wk_ref[...])
        v = _dot(xv, wv_ref[...])
        wl = w0 + _dot(jnp.tanh(lo_w).astype(BF16), w2_ref[...])
        w = -_softplus(-wl) - 0.5
        a = _sigmoid(a0 + _dot(lo_a.astype(BF16), a2_ref[...]))
        g = _dot(_sigmoid(lo_g).astype(BF16), g2_ref[...])
        if has_vres:
            v0 = vec[4:5]
            gate = _sigmoid(v0 + _dot(lo_v.astype(BF16), v2_ref[...]))
            v = v + (vf_ref[0, rows, :] - v) * gate
        kk = k * k_k
        ss = _head_sum(kk * kk, bd_ref[...])
        kk = kk * lax.rsqrt(jnp.maximum(ss, 1e-24))
        r_out[0, rows, :] = r
        lw_out[0, rows, :] = -jnp.exp(w)
        k_out[0, rows, :] = k * (1.0 + (a - 1.0) * k_a)
        v_out[0, rows, :] = v
        a_out[0, rows, :] = -kk
        b_out[0, rows, :] = kk * a
        g_out[0, rows, :] = g


def _block_diag_ones(n, blk):
    idx = jnp.arange(n) // blk
    return (idx[:, None] == idx[None, :]).astype(BF16)


def _rwkv_proj(h, gn, mu, vec, ws, vres, tm):
    B, S, D = h.shape
    has_vres = vres is not None
    tok = pl.BlockSpec((1, tm, D), lambda b, i: (b, i, 0))
    prev = pl.BlockSpec((1, SUBLANES, D), lambda b, i: (b, jnp.maximum(i * (tm // SUBLANES) - 1, 0), 0))
    args = [h, h, gn.reshape(1, D), mu, vec] + list(ws) + [_block_diag_ones(MXU_COLS, RWKV_HEAD)]
    specs = [tok, prev, _full((1, D)), _full(mu.shape), _full(vec.shape)] + [_full(w.shape) for w in ws]
    specs.append(_full((MXU_COLS, MXU_COLS)))
    if has_vres:
        v1, v2, v_first = vres
        args += [v1, v2, v_first]
        specs += [_full(v1.shape), _full(v2.shape), tok]
    out = jax.ShapeDtypeStruct((B, S, D), F32)
    return pl.pallas_call(
        functools.partial(_rwkv_proj_kernel, has_vres, tm),
        grid=(B, S // tm),
        in_specs=specs,
        out_specs=[tok] * 7,
        out_shape=[out] * 7,
        compiler_params=_cparams("parallel", "parallel"),
        name="rwkv_proj",
    )(*args)


def _wkv_kernel(r_ref, lw_ref, k_ref, v_ref, a_ref, b_ref, g_ref, vec_ref, tri_ref, bd_ref,
                y_ref, h_scr, y_scr):
    L = WKV_CHUNK
    P = 2 * L
    nb, _, D = r_ref.shape
    npair = D // LANES

    @pl.when(pl.program_id(1) == 0)
    def _():
        h_scr[...] = jnp.zeros_like(h_scr)

    lane = lax.broadcasted_iota(jnp.int32, (1, LANES), 1)
    first = lane < RWKV_HEAD
    row = lax.broadcasted_iota(jnp.int32, (P, P), 0)
    col = lax.broadcasted_iota(jnp.int32, (P, P), 1)
    lower = row > col
    lower_incl = row >= col
    eye = (row == col).astype(F32)

    def stack(x):
        return jnp.concatenate([jnp.where(first, x, 0.0), jnp.where(first, 0.0, x)], axis=0).astype(BF16)

    def stack2(x, y, sl):
        return jnp.concatenate([stack(x[:, sl]), stack(y[:, sl])], axis=0)

    ar, bk, blkl, v_s, h0, w_end = [], [], [], [], [], []
    for bi in range(nb):
        lw = lw_ref[bi]
        hi = lw.astype(BF16)
        rest = lw - hi.astype(F32)
        mid = rest.astype(BF16)
        low = (rest - mid.astype(F32)).astype(BF16)
        tri = tri_ref[...]
        cum = _dot(tri, hi) + _dot(tri, mid) + _dot(tri, low)
        e_pos = jnp.exp(cum)
        e_neg = jnp.exp(-cum)
        rt = r_ref[bi] * e_pos
        at = a_ref[bi] * jnp.exp(cum - lw)
        kt = k_ref[bi] * e_neg
        bt = b_ref[bi] * e_neg
        we = e_pos[L - 1:L, :]
        kl = kt * we
        bl = bt * we
        v = v_ref[bi]
        for p in range(npair):
            sl = slice(p * LANES, (p + 1) * LANES)
            ar.append(stack2(at, rt, sl))
            bk.append(stack2(bt, kt, sl))
            blkl.append(stack2(bl, kl, sl))
            v_s.append(stack(v[:, sl]))
            h0.append(h_scr[bi * npair + p])
            w_end.append(we[:, sl])

    sc = [lax.dot_general(x, y, _NT, preferred_element_type=F32) for x, y in zip(ar, bk)]
    m_abt = [jnp.transpose(jnp.where(lower, s[:P, :P], 0.0)) for s in sc]
    m_rb = [jnp.where(lower_incl, s[P:, :P], 0.0).astype(BF16) for s in sc]
    m_k = [jnp.concatenate([jnp.where(lower, s[:P, P:], 0.0), jnp.where(lower_incl, s[P:, P:], 0.0)],
                           axis=0).astype(BF16) for s in sc]
    xy = [_dot(jnp.concatenate([x, m], axis=1), jnp.concatenate([h.astype(BF16), vs], axis=0))
          for x, m, h, vs in zip(ar, m_k, h0, v_s)]
    t_t = [eye + m for m in m_abt]
    a_cur = [m.astype(BF16) for m in m_abt]
    a_cur = [_dot(m, m).astype(BF16) for m in a_cur]
    n = 4
    while n < L:
        res = [_dot(m, jnp.concatenate([m, t.astype(BF16)], axis=1)) for m, t in zip(a_cur, t_t)]
        t_t = [t + x[:, P:] for t, x in zip(t_t, res)]
        a_cur = [x[:, :P].astype(BF16) for x in res]
        n *= 2
    t_t = [t + _dot(m, t.astype(BF16)) for t, m in zip(t_t, a_cur)]
    u_s = [lax.dot_general(t.astype(BF16), x[:P].astype(BF16), _TN,
                           preferred_element_type=F32).astype(BF16) for t, x in zip(t_t, xy)]
    ys = [x[P:] + _dot(m, u) for x, m, u in zip(xy, m_rb, u_s)]
    dh = [lax.dot_general(w, jnp.concatenate([u, vs], axis=0), _TN, preferred_element_type=F32)
          for w, u, vs in zip(blkl, u_s, v_s)]
    for j in range(nb * npair):
        bi, p = divmod(j, npair)
        y_scr[bi, :, p * LANES:(p + 1) * LANES] = ys[j][:L] + ys[j][L:]
        decay = jnp.transpose(jnp.broadcast_to(w_end[j], (LANES, LANES)))
        h_scr[j] = h0[j] * decay + dh[j]

    vec = vec_ref[...]
    r_k, ln_w, ln_b = vec[0:1], vec[1:2], vec[2:3]
    bd = bd_ref[...]
    inv_n = 1.0 / RWKV_HEAD
    for bi in range(nb):
        y = y_scr[bi]
        d = y - _head_sum(y, bd) * inv_n
        var = _head_sum(d * d, bd) * inv_n
        yn = d * lax.rsqrt(var + GN_EPS) * ln_w + ln_b
        bonus = _head_sum(r_ref[bi] * k_ref[bi] * r_k, bd) * v_ref[bi]
        y_ref[bi] = ((yn + bonus) * g_ref[bi]).astype(y_ref.dtype)


def _wkv(r, lw, k, v, a, b, g, vec, nb):
    B, S, D = r.shape
    L = WKV_CHUNK
    tok = pl.BlockSpec((nb, L, D), lambda bi, c: (bi, c, 0))
    tri = (jnp.arange(L)[:, None] >= jnp.arange(L)[None, :]).astype(BF16)
    return pl.pallas_call(
        _wkv_kernel,
        grid=(B // nb, S // L),
        in_specs=[tok] * 7 + [_full(vec.shape), _full((L, L)), _full((MXU_COLS, MXU_COLS))],
        out_specs=tok,
        out_shape=jax.ShapeDtypeStruct((B, S, D), BF16),
        scratch_shapes=[pltpu.VMEM((nb * D // LANES, LANES, LANES), F32), pltpu.VMEM((nb, L, D), F32)],
        compiler_params=_cparams("parallel", "arbitrary"),
        name="wkv",
    )(r, lw, k, v, a, b, g, vec, tri, _block_diag_ones(MXU_COLS, RWKV_HEAD))


def _rwkv_layer(h, gn, mu, wr, wk, wv, w0, w1, w2, a0, a1, a2, g1, g2, k_k, k_a, r_k, lnx_w, lnx_b,
                vres, tm):
    B, S, D = h.shape
    rows = [w0, a0, k_k, k_a]
    extra = None
    if vres is not None:
        v0, v1, v2, v_first = vres
        rows.append(v0)
        extra = (v1.astype(BF16), v2.astype(BF16), v_first)
    vec = jnp.stack(rows + [jnp.zeros_like(w0)] * (SUBLANES - len(rows)))
    mu8 = jnp.concatenate([mu, jnp.zeros((SUBLANES - mu.shape[0], D), F32)])
    ws = [w.astype(BF16) for w in (wr, wk, wv, w1, w2, a1, a2, g1, g2)]
    r, lw, k, v, a, b, g = _rwkv_proj(h, gn, mu8, vec, ws, extra, tm)
    vec2 = jnp.stack([r_k.reshape(D), lnx_w, lnx_b] + [jnp.zeros_like(w0)] * (SUBLANES - 3))
    return _wkv(r, lw, k, v, a, b, g, vec2, _tile(B, WKV_ROWS)), v


def _glu_kernel(h_ref, g_ref, w_ref, b_ref, z_ref):
    C = z_ref.shape[-1]
    u = _rms(h_ref[...], g_ref[...]).astype(BF16)
    z = _dot(u, w_ref[...]) + b_ref[...]
    z_ref[...] = z[:, :C] * _sigmoid(z[:, C:])


def _glu(h2, g, w1, b1, tm):
    T, D = h2.shape
    N = w1.shape[1]
    return pl.pallas_call(
        _glu_kernel,
        grid=(T // tm,),
        in_specs=[pl.BlockSpec((tm, D), lambda i: (i, 0)), _full((1, D)), _full((D, N)), _full((1, N))],
        out_specs=pl.BlockSpec((tm, N // 2), lambda i: (i, 0)),
        out_shape=jax.ShapeDtypeStruct((T, N // 2), F32),
        compiler_params=_cparams("parallel"),
        name="conv_glu",
    )(h2, g.reshape(1, D), w1, b1.reshape(1, N))


_CONV_HALO = 32


_CONV_ROWS = 64
_CONV_COLS = 128


def _dwconv_kernel(tm, z_ref, zp_ref, dw_ref, vec_ref, o_ref, zbuf):
    i = pl.program_id(1)
    C = z_ref.shape[-1]
    zbuf[0:_CONV_HALO, :] = jnp.where(i > 0, zp_ref[0], 0.0)
    zbuf[_CONV_HALO:, :] = z_ref[0]
    vec = vec_ref[...]
    bdw, ln_w, ln_b = vec[0:1], vec[1:2], vec[2:3]
    off = _CONV_HALO - (CONV_WIDTH - 1)
    span = _CONV_ROWS + _CONV_HALO
    for r0 in range(0, tm, _CONV_ROWS):
        parts = []
        for c0 in range(0, C, _CONV_COLS):
            cols = slice(c0, c0 + _CONV_COLS)
            acc = None
            win = zbuf[r0:r0 + span, cols]
            for res in range(SUBLANES):
                shifted = pltpu.roll(win, span - res, axis=0) if res else win
                for o in range(off + (res - off) % SUBLANES, off + CONV_WIDTH, SUBLANES):
                    q = o - res
                    term = shifted[q:q + _CONV_ROWS, :] * dw_ref[o - off:o - off + 1, cols]
                    acc = term if acc is None else acc + term
            parts.append(acc)
        acc = jnp.concatenate(parts, axis=-1) + bdw
        mean = jnp.mean(acc, axis=-1, keepdims=True)
        d = acc - mean
        var = jnp.mean(d * d, axis=-1, keepdims=True)
        y = d * lax.rsqrt(var + NORM_EPS) * ln_w + ln_b
        o_ref[0, r0:r0 + _CONV_ROWS, :] = (y * _sigmoid(y)).astype(o_ref.dtype)


def _dwconv(z, dw, vec, tm):
    B, S, C = z.shape
    tok = pl.BlockSpec((1, tm, C), lambda b, i: (b, i, 0))
    prev = pl.BlockSpec((1, _CONV_HALO, C), lambda b, i: (b, jnp.maximum(i * (tm // _CONV_HALO) - 1, 0), 0))
    return pl.pallas_call(
        functools.partial(_dwconv_kernel, tm),
        grid=(B, S // tm),
        in_specs=[tok, prev, _full(dw.shape), _full(vec.shape)],
        out_specs=tok,
        out_shape=jax.ShapeDtypeStruct((B, S, C), BF16),
        scratch_shapes=[pltpu.VMEM((_CONV_HALO + tm, C), F32)],
        compiler_params=_cparams("parallel", "parallel"),
        name="conv_dw",
    )(z, z, dw, vec)


def _conv_layer(h, gn, w1, b1, dw, bdw, ln_w, ln_b, tm):
    B, S, D = h.shape
    z = _glu(h.reshape(B * S, D), gn, w1.astype(BF16), b1, tm).reshape(B, S, -1)
    dw32 = jnp.concatenate([dw, jnp.zeros((32 - CONV_WIDTH, dw.shape[1]), F32)])
    vec = jnp.stack([bdw, ln_w, ln_b] + [jnp.zeros_like(bdw)] * (SUBLANES - 3))
    return _dwconv(z, dw32, vec, tm)


def _rope_table_kernel(pos_ref, frq_ref, cos_ref, sin_ref):
    ang = pos_ref[...] * frq_ref[...]
    cos_ref[...] = jnp.cos(ang)
    sin_ref[...] = jnp.sin(ang)


def _rope_tables(positions):
    B, S = positions.shape
    half = QK_ROPE // 2
    per_row = LANES // half
    rows = B * S // per_row
    inv_freq = ROPE_THETA ** (-jnp.arange(0, QK_ROPE, 2, dtype=F32) / QK_ROPE)
    pos = jnp.repeat(positions.astype(F32).reshape(rows, per_row), half, axis=1)
    frq = jnp.tile(inv_freq, per_row).reshape(1, LANES)
    tr = min(rows, 512)
    cos, sin = pl.pallas_call(
        _rope_table_kernel,
        grid=(rows // tr,),
        in_specs=[pl.BlockSpec((tr, LANES), lambda i: (i, 0)), _full((1, LANES))],
        out_specs=[pl.BlockSpec((tr, LANES), lambda i: (i, 0))] * 2,
        out_shape=[jax.ShapeDtypeStruct((rows, LANES), F32)] * 2,
        compiler_params=_cparams("parallel"),
        name="rope_table",
    )(pos, frq)
    cos = cos.reshape(B * S, half)
    sin = sin.reshape(B * S, half)
    cos_t = jnp.tile(cos, (1, LANES // half))
    sin_t = jnp.tile(jnp.concatenate([-sin, sin], axis=1), (1, LANES // QK_ROPE))
    return cos_t, sin_t


def _mla_proj_kernel(scale, h_ref, gn_ref, wd_ref, gq_ref, gkv_ref, wqn_ref, wqr_ref, wqs_ref, wkn_ref, wv_ref,
                     cos_ref, sin_ref, qn_out, qr_out, kn_out, kr_out, v_out):
    u = _rms(h_ref[...], gn_ref[...]).astype(BF16)
    c = _dot(u, wd_ref[...])
    cq = _rms(c[:, :Q_LORA], gq_ref[...]).astype(BF16)
    ckv = _rms(c[:, Q_LORA:Q_LORA + KV_LORA], gkv_ref[...]).astype(BF16)
    o = Q_LORA + KV_LORA
    cos = cos_ref[...]
    sin = sin_ref[...]
    kr_out[...] = (c[:, o:o + LANES] * cos + c[:, o + LANES:o + 2 * LANES] * sin).astype(BF16)
    reps = qr_out.shape[-1] // LANES
    cos_w = jnp.concatenate([cos] * reps, axis=-1)
    sin_w = jnp.concatenate([sin] * reps, axis=-1)
    qn_out[...] = (_dot(cq, wqn_ref[...]) * scale).astype(BF16)
    qr = _dot(cq, wqr_ref[...]) * cos_w + _dot(cq, wqs_ref[...]) * sin_w
    qr_out[...] = (qr * scale).astype(BF16)
    kn_out[...] = _dot(ckv, wkn_ref[...]).astype(BF16)
    v_out[...] = _dot(ckv, wv_ref[...]).astype(BF16)


def _swap_halves(w):
    half = QK_ROPE // 2
    return jnp.concatenate([w[..., half:], w[..., :half]], axis=-1)


def _mla_proj(h2, gn, wd, gq, wuq, gkv, wukv, cos_t, sin_t, tm):
    T, D = h2.shape
    H = MLA_HEADS
    npair = H // 2
    HD = H * QK_NOPE
    kr_w = wd[:, Q_LORA + KV_LORA:]
    zpad = jnp.zeros((D, LANES - 2 * QK_ROPE), F32)
    wd2 = jnp.concatenate([wd[:, :Q_LORA + KV_LORA], kr_w, kr_w, zpad,
                           _swap_halves(kr_w), _swap_halves(kr_w), zpad], axis=1).astype(BF16)
    wq = wuq.reshape(Q_LORA, H, QK_NOPE + QK_ROPE)
    wqn = wq[:, :, :QK_NOPE].reshape(Q_LORA, HD).astype(BF16)
    wq_rope = wq[:, :, QK_NOPE:]

    def pair_pad(w):
        w = w.reshape(Q_LORA, npair, 2 * QK_ROPE)
        w = jnp.concatenate([w, jnp.zeros((Q_LORA, npair, LANES - 2 * QK_ROPE), F32)], axis=-1)
        return w.reshape(Q_LORA, npair * LANES).astype(BF16)

    wqr = pair_pad(wq_rope)
    wqs = pair_pad(_swap_halves(wq_rope))
    wkv = wukv.reshape(KV_LORA, H, QK_NOPE + V_DIM)
    wkn = wkv[:, :, :QK_NOPE].reshape(KV_LORA, HD).astype(BF16)
    wv = wkv[:, :, QK_NOPE:].reshape(KV_LORA, H * V_DIM).astype(BF16)
    scale = math.log2(math.e) / math.sqrt(QK_NOPE + QK_ROPE)
    ws = [wd2, gq.reshape(1, -1), gkv.reshape(1, -1), wqn, wqr, wqs, wkn, wv]
    tokw = lambda n: pl.BlockSpec((tm, n), lambda i: (i, 0))
    return pl.pallas_call(
        functools.partial(_mla_proj_kernel, scale),
        grid=(T // tm,),
        in_specs=[tokw(D), _full((1, D))] + [_full(w.shape) for w in ws] + [tokw(LANES), tokw(LANES)],
        out_specs=[tokw(HD), tokw(npair * LANES), tokw(HD), tokw(LANES), tokw(H * V_DIM)],
        out_shape=[jax.ShapeDtypeStruct((T, HD), BF16), jax.ShapeDtypeStruct((T, npair * LANES), BF16),
                   jax.ShapeDtypeStruct((T, HD), BF16), jax.ShapeDtypeStruct((T, LANES), BF16),
                   jax.ShapeDtypeStruct((T, H * V_DIM), BF16)],
        compiler_params=_cparams("parallel"),
        name="mla_proj",
    )(h2, gn.reshape(1, D), *ws, cos_t, sin_t)


def _attn_kernel(tq, qn_ref, qr_ref, kn_ref, kr_ref, v_ref, o_ref):
    S = qn_ref.shape[1]
    lane = lax.broadcasted_iota(jnp.int32, (1, 2 * LANES), 1)
    head0 = (lane < QK_NOPE) | ((lane >= LANES) & (lane < LANES + QK_ROPE))
    head1 = ((lane >= QK_NOPE) & (lane < LANES)) | ((lane >= LANES + QK_ROPE) & (lane < LANES + 2 * QK_ROPE))
    row = lax.broadcasted_iota(jnp.int32, (2 * tq, tq), 0) % tq
    col = lax.broadcasted_iota(jnp.int32, (2 * tq, tq), 1)
    causal = col <= row
    out_lane = lax.broadcasted_iota(jnp.int32, (1, LANES), 1)

    def scores(i):
        q0, n = i * tq, (i + 1) * tq
        qc = jnp.concatenate([qn_ref[0, q0:n, :], qr_ref[0, q0:n, :]], axis=-1)
        zero = jnp.zeros_like(qc)
        qs = jnp.concatenate([jnp.where(head0, qc, zero), jnp.where(head1, qc, zero)], axis=0)
        kc = jnp.concatenate([kn_ref[0, :n, :], kr_ref[0, :n, :]], axis=-1)
        return lax.dot_general(qs, kc, _NT, preferred_element_type=F32)

    nq = S // tq
    s_next = scores(0)
    for i in range(nq):
        q0, n = i * tq, (i + 1) * tq
        s = s_next
        if i + 1 < nq:
            s_next = scores(i + 1)
        s_diag = jnp.where(causal, s[:, q0:], -1e30)
        s = jnp.concatenate([s[:, :q0], s_diag], axis=-1) if i else s_diag
        m = jnp.max(s, axis=-1, keepdims=True)
        p = jnp.exp2(s - m)
        l = jnp.sum(p, axis=-1, keepdims=True)
        o = _dot(p.astype(BF16), v_ref[0, :n, :]) / l
        o_ref[0, q0:n, :] = jnp.where(out_lane < V_DIM, o[:tq], o[tq:]).astype(o_ref.dtype)


def _attention(qn, qr, kn, kr, v, tq):
    B, S, HD = qn.shape
    npair = HD // LANES
    spec = pl.BlockSpec((1, S, LANES), lambda b, p: (b, 0, p))
    return pl.pallas_call(
        functools.partial(_attn_kernel, tq),
        grid=(B, npair),
        in_specs=[spec, spec, spec, pl.BlockSpec((1, S, LANES), lambda b, p: (b, 0, 0)), spec],
        out_specs=spec,
        out_shape=jax.ShapeDtypeStruct((B, S, HD), BF16),
        compiler_params=_cparams("parallel", "parallel"),
        name="mla_attn",
    )(qn, qr, kn, kr, v)


def _mla_layer(h, positions, gn, wd, gq, wuq, gkv, wukv, tm, tq):
    B, S, D = h.shape
    cos_t, sin_t = _rope_tables(positions)
    qn, qr, kn, kr, v = _mla_proj(h.reshape(B * S, D), gn, wd, gq, wuq, gkv, wukv, cos_t, sin_t, tm)
    sh = lambda t: t.reshape(B, S, t.shape[-1])
    return _attention(sh(qn), sh(qr), sh(kn), sh(kr), sh(v), tq)


def _tile(n, pref):
    t = min(n, pref)
    while n % t:
        t //= 2
    return t


def kernel(x, positions, norm_mix, norm_ffn, norm_final, rw_mu, rw_wr, rw_wk, rw_wv, rw_wo, rw_w0, rw_w1, rw_w2, rw_a0, rw_a1, rw_a2, rw_g1, rw_g2, rw_kk, rw_ka, rw_rk, rw_lnx_w, rw_lnx_b, rw_v0, rw_v1, rw_v2, cv_w1, cv_b1, cv_dw, cv_bdw, cv_ln_w, cv_ln_b, cv_w2, cv_b2, ml_wd, ml_gq, ml_wuq, ml_gkv, ml_wukv, ml_wo, ff_w1, ff_w2):
    B, S, D = x.shape
    depth = norm_mix.shape[0]
    tm = _tile(S, 512)
    tq = _tile(S, 256)
    tf = _tile(ff_w1.shape[-1], 1024)
    h = x
    v_first = None
    zero_bias = jnp.zeros((D,), F32)
    ff_w1_bf = ff_w1.astype(BF16)
    ff_w2_bf = ff_w2.astype(BF16)
    ia = ib = ic = 0
    for i in range(depth):
        kind = i % 3
        if kind == 0:
            vres = None if ia == 0 else (rw_v0[ia - 1], rw_v1[ia - 1], rw_v2[ia - 1], v_first)
            y, v = _rwkv_layer(h, norm_mix[i], rw_mu[ia], rw_wr[ia], rw_wk[ia], rw_wv[ia],
                               rw_w0[ia], rw_w1[ia], rw_w2[ia], rw_a0[ia], rw_a1[ia], rw_a2[ia],
                               rw_g1[ia], rw_g2[ia], rw_kk[ia], rw_ka[ia], rw_rk[ia],
                               rw_lnx_w[ia], rw_lnx_b[ia], vres, tm)
            wo, bo = rw_wo[ia], zero_bias
            if ia == 0:
                v_first = v
            ia += 1
        elif kind == 1:
            y = _conv_layer(h, norm_mix[i], cv_w1[ib], cv_b1[ib], cv_dw[ib], cv_bdw[ib],
                            cv_ln_w[ib], cv_ln_b[ib], tm)
            wo, bo = cv_w2[ib], cv_b2[ib]
            ib += 1
        else:
            y = _mla_layer(h, positions, norm_mix[i], ml_wd[ic], ml_gq[ic], ml_wuq[ic],
                           ml_gkv[ic], ml_wukv[ic], tm, tq)
            wo, bo = ml_wo[ic], zero_bias
            ic += 1
        h = _mlp(h.reshape(B * S, D), y.reshape(B * S, -1), wo.astype(BF16), bo, norm_ffn[i],
                 ff_w1_bf, ff_w2_bf, i, norm_final, i == depth - 1, tm, tf).reshape(B, S, D)
    return h
```

```python
import functools
import math

import jax
import jax.numpy as jnp
from jax import lax
from jax.experimental import pallas as pl
from jax.experimental.pallas import tpu as pltpu

F32 = jnp.float32
BF16 = jnp.bfloat16

NORM_EPS = 1e-5
GN_EPS = 64e-5
RWKV_HEAD = 64
CONV_WIDTH = 31
MLA_HEADS = 16
QK_NOPE = 64
QK_ROPE = 32
V_DIM = 64
Q_LORA = 384
KV_LORA = 256
ROPE_THETA = 10000.0

LANES = 128
SUBLANES = 8
MXU_COLS = 256
WKV_CHUNK = 64
WKV_ROWS = 4
RWKV_PROJ_SPLIT = 1
VMEM_LIMIT_BYTES = 56 * 1024 * 1024

_NT = (((1,), (1,)), ((), ()))
_TN = (((0,), (0,)), ((), ()))


def _cparams(*sem):
    return pltpu.CompilerParams(dimension_semantics=sem, vmem_limit_bytes=VMEM_LIMIT_BYTES)


def _dot(a, b):
    return jnp.dot(a, b, preferred_element_type=F32)


def _rms(x, g):
    ms = jnp.mean(x * x, axis=-1, keepdims=True)
    return x * lax.rsqrt(ms + NORM_EPS) * g


def _sigmoid(x):
    return 1.0 / (1.0 + jnp.exp(-x))


def _softplus(x):
    return jnp.maximum(x, 0.0) + jnp.log(1.0 + jnp.exp(-jnp.abs(x)))


def _head_sum(x, bd):
    outs = []
    for c in range(x.shape[-1] // MXU_COLS):
        xc = x[:, c * MXU_COLS:(c + 1) * MXU_COLS]
        hi = xc.astype(BF16)
        lo = (xc - hi.astype(F32)).astype(BF16)
        outs.append(_dot(hi, bd) + _dot(lo, bd))
    return jnp.concatenate(outs, axis=-1)


def _full(shape):
    n = len(shape)
    return pl.BlockSpec(shape, lambda *_: (0,) * n)


def _mlp_kernel(final_norm, tf, h_ref, y_ref, wo_ref, bo_ref, g_ref, w1_ref, w2_ref, gf_ref, o_ref, a_scr):
    h = h_ref[...] + bo_ref[...] + _dot(y_ref[...], wo_ref[...])
    u = _rms(h, g_ref[...]).astype(BF16)
    for c in range(a_scr.shape[1] // tf):
        cols = slice(c * tf, (c + 1) * tf)
        a = _dot(u, w1_ref[:, cols])
        a_scr[:, cols] = jnp.square(jnp.maximum(a, 0.0)).astype(BF16)
    out = h + _dot(a_scr[...], w2_ref[...])
    if final_norm:
        out = _rms(out, gf_ref[...])
    o_ref[...] = out


def _resident(shape):
    n = len(shape)
    return pl.BlockSpec(shape, lambda *_: (0,) * n, pipeline_mode=pl.Buffered(1))


def _mlp(h2, y2, wo, bo, g, w1_all, w2_all, layer, gf, final_norm, tm, tf):
    T, D = h2.shape
    K = y2.shape[1]
    F = w1_all.shape[2]
    one = pl.Buffered(1)
    return pl.pallas_call(
        functools.partial(_mlp_kernel, final_norm, tf),
        grid=(T // tm,),
        in_specs=[
            pl.BlockSpec((tm, D), lambda i: (i, 0)),
            pl.BlockSpec((tm, K), lambda i: (i, 0)),
            _resident((K, D)),
            _full((1, D)),
            _full((1, D)),
            pl.BlockSpec((None, D, F), lambda i: (layer, 0, 0), pipeline_mode=one),
            pl.BlockSpec((None, F, D), lambda i: (layer, 0, 0), pipeline_mode=one),
            _full((1, D)),
        ],
        out_specs=pl.BlockSpec((tm, D), lambda i: (i, 0)),
        out_shape=jax.ShapeDtypeStruct((T, D), F32),
        scratch_shapes=[pltpu.VMEM((tm, F), BF16)],
        compiler_params=_cparams("parallel"),
        name="mlp",
    )(h2, y2, wo, bo.reshape(1, D), g.reshape(1, D), w1_all, w2_all, gf.reshape(1, D))


def _rwkv_proj_kernel(has_vres, tm, *refs):
    if has_vres:
        (h_ref, hp_ref, gn_ref, mu_ref, vec_ref, wr_ref, wk_ref, wv_ref, w1_ref, w2_ref, a1_ref, a2_ref,
         g1_ref, g2_ref, bd_ref, v1_ref, v2_ref, vf_ref,
         r_out, lw_out, k_out, v_out, a_out, b_out, g_out) = refs
    else:
        (h_ref, hp_ref, gn_ref, mu_ref, vec_ref, wr_ref, wk_ref, wv_ref, w1_ref, w2_ref, a1_ref, a2_ref,
         g1_ref, g2_ref, bd_ref,
         r_out, lw_out, k_out, v_out, a_out, b_out, g_out) = refs
    i = pl.program_id(1)
    gn = gn_ref[...]
    mu = mu_ref[...]
    vec = vec_ref[...]
    w0, a0, k_k, k_a = vec[0:1], vec[1:2], vec[2:3], vec[3:4]
    prev = _rms(hp_ref[0], gn)[SUBLANES - 1:SUBLANES, :]
    prev = jnp.where(i > 0, prev, 0.0)
    nrows = tm // RWKV_PROJ_SPLIT
    row = lax.broadcasted_iota(jnp.int32, (nrows, 1), 0)
    for s in range(RWKV_PROJ_SPLIT):
        rows = slice(s * nrows, (s + 1) * nrows)
        u = _rms(h_ref[0, rows, :], gn)
        shifted = jnp.where(row == 0, prev, pltpu.roll(u, 1, axis=0))
        prev = u[nrows - 1:nrows, :]
        xx = shifted - u

        def mix(j):
            return (u + xx * mu[j:j + 1, :]).astype(BF16)

        xr, xw, xk, xv, xa, xg = [mix(j) for j in range(6)]
        lo_w = _dot(xw, w1_ref[...])
        lo_a = _dot(xa, a1_ref[...])
        lo_g = _dot(xg, g1_ref[...])
        if has_vres:
            lo_v = _dot(xv, v1_ref[...])
        r = _dot(xr, wr_ref[...])
        k = _dot(xk, wk_ref[...])
        v = _dot(xv, wv_ref[...])
        wl = w0 + _dot(jnp.tanh(lo_w).astype(BF16), w2_ref[...])
        w = -_softplus(-wl) - 0.5
        a = _sigmoid(a0 + _dot(lo_a.astype(BF16), a2_ref[...]))
        g = _dot(_sigmoid(lo_g).astype(BF16), g2_ref[...])
        if has_vres:
            v0 = vec[4:5]
            gate = _sigmoid(v0 + _dot(lo_v.astype(BF16), v2_ref[...]))
            v = v + (vf_ref[0, rows, :] - v) * gate
        kk = k * k_k
        ss = _head_sum(kk * kk, bd_ref[...])
        kk = kk * lax.rsqrt(jnp.maximum(ss, 1e-24))
        r_out[0, rows, :] = r
        lw_out[0, rows, :] = -jnp.exp(w)
        k_out[0, rows, :] = k * (1.0 + (a - 1.0) * k_a)
        v_out[0, rows, :] = v
        a_out[0, rows, :] = -kk
        b_out[0, rows, :] = kk * a
        g_out[0, rows, :] = g


def _block_diag_ones(n, blk):
    idx = jnp.arange(n) // blk
    return (idx[:, None] == idx[None, :]).astype(BF16)


def _rwkv_proj(h, gn, mu, vec, ws, vres, tm):
    B, S, D = h.shape
    has_vres = vres is not None
    tok = pl.BlockSpec((1, tm, D), lambda b, i: (b, i, 0))
    prev = pl.BlockSpec((1, SUBLANES, D), lambda b, i: (b, jnp.maximum(i * (tm // SUBLANES) - 1, 0), 0))
    args = [h, h, gn.reshape(1, D), mu, vec] + list(ws) + [_block_diag_ones(MXU_COLS, RWKV_HEAD)]
    specs = [tok, prev, _full((1, D)), _full(mu.shape), _full(vec.shape)] + [_full(w.shape) for w in ws]
    specs.append(_full((MXU_COLS, MXU_COLS)))
    if has_vres:
        v1, v2, v_first = vres
        args += [v1, v2, v_first]
        specs += [_full(v1.shape), _full(v2.shape), tok]
    out = jax.ShapeDtypeStruct((B, S, D), F32)
    return pl.pallas_call(
        functools.partial(_rwkv_proj_kernel, has_vres, tm),
        grid=(B, S // tm),
        in_specs=specs,
        out_specs=[tok] * 7,
        out_shape=[out] * 7,
        compiler_params=_cparams("parallel", "parallel"),
        name="rwkv_proj",
    )(*args)


def _wkv_kernel(r_ref, lw_ref, k_ref, v_ref, a_ref, b_ref, g_ref, vec_ref, tri_ref, bd_ref,
                y_ref, h_scr, y_scr):
    L = WKV_CHUNK
    nb, _, D = r_ref.shape
    npair = D // LANES

    @pl.when(pl.program_id(1) == 0)
    def _():
        h_scr[...] = jnp.zeros_like(h_scr)

    lane = lax.broadcasted_iota(jnp.int32, (1, LANES), 1)
    first = lane < RWKV_HEAD
    t_idx = lax.broadcasted_iota(jnp.int32, (L, LANES), 0)
    s_idx = lax.broadcasted_iota(jnp.int32, (L, LANES), 1) % RWKV_HEAD
    lower = t_idx > s_idx
    lower_incl = t_idx >= s_idx
    eye = (t_idx == s_idx).astype(F32)
    same_head = (lax.broadcasted_iota(jnp.int32, (LANES, LANES), 0) // RWKV_HEAD
                 == lax.broadcasted_iota(jnp.int32, (LANES, LANES), 1) // RWKV_HEAD)

    def stack(x):
        return jnp.concatenate([jnp.where(first, x, 0.0), jnp.where(first, 0.0, x)], axis=0).astype(BF16)

    ar, bk, blkl, v_b, v_s, h0, w_end = [], [], [], [], [], [], []
    for bi in range(nb):
        lw = lw_ref[bi]
        hi = lw.astype(BF16)
        rest = lw - hi.astype(F32)
        mid = rest.astype(BF16)
        low = (rest - mid.astype(F32)).astype(BF16)
        tri = tri_ref[...]
        cum = _dot(tri, hi) + _dot(tri, mid) + _dot(tri, low)
        e_pos = jnp.exp(cum)
        e_neg = jnp.exp(-cum)
        rt = r_ref[bi] * e_pos
        at = a_ref[bi] * jnp.exp(cum - lw)
        kt = k_ref[bi] * e_neg
        bt = b_ref[bi] * e_neg
        we = e_pos[L - 1:L, :]
        kl = kt * we
        bl = bt * we
        v = v_ref[bi]
        for p in range(npair):
            sl = slice(p * LANES, (p + 1) * LANES)
            ar.append(jnp.concatenate([at[:, sl], rt[:, sl]], axis=0).astype(BF16))
            bk.append(jnp.concatenate([stack(bt[:, sl]), stack(kt[:, sl])], axis=0))
            blkl.append(jnp.concatenate([bl[:, sl], kl[:, sl]], axis=0).astype(BF16))
            v_b.append(v[:, sl].astype(BF16))
            v_s.append(stack(v[:, sl]))
            h0.append(h_scr[bi * npair + p])
            w_end.append(we[:, sl])

    sc = [lax.dot_general(x, y, _NT, preferred_element_type=F32) for x, y in zip(ar, bk)]
    m_ab = [jnp.where(lower, s[:L, :LANES], 0.0) for s in sc]
    m_rb = [jnp.where(lower_incl, s[L:, :LANES], 0.0).astype(BF16) for s in sc]
    m_k = [jnp.concatenate([jnp.where(lower, s[:L, LANES:], 0.0), jnp.where(lower_incl, s[L:, LANES:], 0.0)],
                           axis=0).astype(BF16) for s in sc]
    xy = [_dot(jnp.concatenate([x, m], axis=1), jnp.concatenate([h.astype(BF16), vs], axis=0))
          for x, m, h, vs in zip(ar, m_k, h0, v_s)]
    t_inv = [eye + m for m in m_ab]
    a_cur = [_dot(m.astype(BF16), stack(m)) for m in m_ab]
    n = 4
    while n < L:
        res = [_dot(m.astype(BF16), jnp.concatenate([stack(m), stack(t)], axis=1)) for m, t in zip(a_cur, t_inv)]
        t_inv = [t + x[:, LANES:] for t, x in zip(t_inv, res)]
        a_cur = [x[:, :LANES] for x in res]
        n *= 2
    t_inv = [t + _dot(m.astype(BF16), stack(t)) for t, m in zip(t_inv, a_cur)]
    u = [_dot(t.astype(BF16), stack(x[:L])) for t, x in zip(t_inv, xy)]
    ys = [x[L:] + _dot(m, stack(uu)) for x, m, uu in zip(xy, m_rb, u)]
    dh = [lax.dot_general(w, jnp.concatenate([uu.astype(BF16), vb], axis=0), _TN, preferred_element_type=F32)
          for w, uu, vb in zip(blkl, u, v_b)]
    for j in range(nb * npair):
        bi, p = divmod(j, npair)
        y_scr[bi, :, p * LANES:(p + 1) * LANES] = ys[j]
        decay = jnp.transpose(jnp.broadcast_to(w_end[j], (LANES, LANES)))
        h_scr[j] = h0[j] * decay + jnp.where(same_head, dh[j], 0.0)

    vec = vec_ref[...]
    r_k, ln_w, ln_b = vec[0:1], vec[1:2], vec[2:3]
    bd = bd_ref[...]
    inv_n = 1.0 / RWKV_HEAD
    for bi in range(nb):
        y = y_scr[bi]
        d = y - _head_sum(y, bd) * inv_n
        var = _head_sum(d * d, bd) * inv_n
        yn = d * lax.rsqrt(var + GN_EPS) * ln_w + ln_b
        bonus = _head_sum(r_ref[bi] * k_ref[bi] * r_k, bd) * v_ref[bi]
        y_ref[bi] = ((yn + bonus) * g_ref[bi]).astype(y_ref.dtype)


def _wkv(r, lw, k, v, a, b, g, vec, nb):
    B, S, D = r.shape
    L = WKV_CHUNK
    tok = pl.BlockSpec((nb, L, D), lambda bi, c: (bi, c, 0))
    tri = (jnp.arange(L)[:, None] >= jnp.arange(L)[None, :]).astype(BF16)
    return pl.pallas_call(
        _wkv_kernel,
        grid=(B // nb, S // L),
        in_specs=[tok] * 7 + [_full(vec.shape), _full((L, L)), _full((MXU_COLS, MXU_COLS))],
        out_specs=tok,
        out_shape=jax.ShapeDtypeStruct((B, S, D), BF16),
        scratch_shapes=[pltpu.VMEM((nb * D // LANES, LANES, LANES), F32), pltpu.VMEM((nb, L, D), F32)],
        compiler_params=_cparams("parallel", "arbitrary"),
        name="wkv",
    )(r, lw, k, v, a, b, g, vec, tri, _block_diag_ones(MXU_COLS, RWKV_HEAD))


def _rwkv_layer(h, gn, mu, wr, wk, wv, w0, w1, w2, a0, a1, a2, g1, g2, k_k, k_a, r_k, lnx_w, lnx_b,
                vres, tm):
    B, S, D = h.shape
    rows = [w0, a0, k_k, k_a]
    extra = None
    if vres is not None:
        v0, v1, v2, v_first = vres
        rows.append(v0)
        extra = (v1.astype(BF16), v2.astype(BF16), v_first)
    vec = jnp.stack(rows + [jnp.zeros_like(w0)] * (SUBLANES - len(rows)))
    mu8 = jnp.concatenate([mu, jnp.zeros((SUBLANES - mu.shape[0], D), F32)])
    ws = [w.astype(BF16) for w in (wr, wk, wv, w1, w2, a1, a2, g1, g2)]
    r, lw, k, v, a, b, g = _rwkv_proj(h, gn, mu8, vec, ws, extra, tm)
    vec2 = jnp.stack([r_k.reshape(D), lnx_w, lnx_b] + [jnp.zeros_like(w0)] * (SUBLANES - 3))
    return _wkv(r, lw, k, v, a, b, g, vec2, _tile(B, WKV_ROWS)), v


def _glu_kernel(h_ref, g_ref, w_ref, b_ref, z_ref):
    C = z_ref.shape[-1]
    u = _rms(h_ref[...], g_ref[...]).astype(BF16)
    z = _dot(u, w_ref[...]) + b_ref[...]
    z_ref[...] = z[:, :C] * _sigmoid(z[:, C:])


def _glu(h2, g, w1, b1, tm):
    T, D = h2.shape
    N = w1.shape[1]
    return pl.pallas_call(
        _glu_kernel,
        grid=(T // tm,),
        in_specs=[pl.BlockSpec((tm, D), lambda i: (i, 0)), _full((1, D)), _full((D, N)), _full((1, N))],
        out_specs=pl.BlockSpec((tm, N // 2), lambda i: (i, 0)),
        out_shape=jax.ShapeDtypeStruct((T, N // 2), F32),
        compiler_params=_cparams("parallel"),
        name="conv_glu",
    )(h2, g.reshape(1, D), w1, b1.reshape(1, N))


_CONV_HALO = 32


_CONV_ROWS = 64
_CONV_COLS = 128


def _dwconv_kernel(tm, z_ref, zp_ref, dw_ref, vec_ref, o_ref, zbuf):
    i = pl.program_id(1)
    C = z_ref.shape[-1]
    zbuf[0:_CONV_HALO, :] = jnp.where(i > 0, zp_ref[0], 0.0)
    zbuf[_CONV_HALO:, :] = z_ref[0]
    vec = vec_ref[...]
    bdw, ln_w, ln_b = vec[0:1], vec[1:2], vec[2:3]
    off = _CONV_HALO - (CONV_WIDTH - 1)
    span = _CONV_ROWS + _CONV_HALO
    for r0 in range(0, tm, _CONV_ROWS):
        parts = []
        for c0 in range(0, C, _CONV_COLS):
            cols = slice(c0, c0 + _CONV_COLS)
            acc = None
            win = zbuf[r0:r0 + span, cols]
            for res in range(SUBLANES):
                shifted = pltpu.roll(win, span - res, axis=0) if res else win
                for o in range(off + (res - off) % SUBLANES, off + CONV_WIDTH, SUBLANES):
                    q = o - res
                    term = shifted[q:q + _CONV_ROWS, :] * dw_ref[o - off:o - off + 1, cols]
                    acc = term if acc is None else acc + term
            parts.append(acc)
        acc = jnp.concatenate(parts, axis=-1) + bdw
        mean = jnp.mean(acc, axis=-1, keepdims=True)
        d = acc - mean
        var = jnp.mean(d * d, axis=-1, keepdims=True)
        y = d * lax.rsqrt(var + NORM_EPS) * ln_w + ln_b
        o_ref[0, r0:r0 + _CONV_ROWS, :] = (y * _sigmoid(y)).astype(o_ref.dtype)


def _dwconv(z, dw, vec, tm):
    B, S, C = z.shape
    tok = pl.BlockSpec((1, tm, C), lambda b, i: (b, i, 0))
    prev = pl.BlockSpec((1, _CONV_HALO, C), lambda b, i: (b, jnp.maximum(i * (tm // _CONV_HALO) - 1, 0), 0))
    return pl.pallas_call(
        functools.partial(_dwconv_kernel, tm),
        grid=(B, S // tm),
        in_specs=[tok, prev, _full(dw.shape), _full(vec.shape)],
        out_specs=tok,
        out_shape=jax.ShapeDtypeStruct((B, S, C), BF16),
        scratch_shapes=[pltpu.VMEM((_CONV_HALO + tm, C), F32)],
        compiler_params=_cparams("parallel", "parallel"),
        name="conv_dw",
    )(z, z, dw, vec)


def _conv_layer(h, gn, w1, b1, dw, bdw, ln_w, ln_b, tm):
    B, S, D = h.shape
    z = _glu(h.reshape(B * S, D), gn, w1.astype(BF16), b1, tm).reshape(B, S, -1)
    dw32 = jnp.concatenate([dw, jnp.zeros((32 - CONV_WIDTH, dw.shape[1]), F32)])
    vec = jnp.stack([bdw, ln_w, ln_b] + [jnp.zeros_like(bdw)] * (SUBLANES - 3))
    return _dwconv(z, dw32, vec, tm)


def _rope_table_kernel(pos_ref, frq_ref, cos_ref, sin_ref):
    ang = pos_ref[...] * frq_ref[...]
    cos_ref[...] = jnp.cos(ang)
    sin_ref[...] = jnp.sin(ang)


def _rope_tables(positions):
    B, S = positions.shape
    half = QK_ROPE // 2
    per_row = LANES // half
    rows = B * S // per_row
    inv_freq = ROPE_THETA ** (-jnp.arange(0, QK_ROPE, 2, dtype=F32) / QK_ROPE)
    pos = jnp.repeat(positions.astype(F32).reshape(rows, per_row), half, axis=1)
    frq = jnp.tile(inv_freq, per_row).reshape(1, LANES)
    tr = min(rows, 512)
    cos, sin = pl.pallas_call(
        _rope_table_kernel,
        grid=(rows // tr,),
        in_specs=[pl.BlockSpec((tr, LANES), lambda i: (i, 0)), _full((1, LANES))],
        out_specs=[pl.BlockSpec((tr, LANES), lambda i: (i, 0))] * 2,
        out_shape=[jax.ShapeDtypeStruct((rows, LANES), F32)] * 2,
        compiler_params=_cparams("parallel"),
        name="rope_table",
    )(pos, frq)
    cos = cos.reshape(B * S, half)
    sin = sin.reshape(B * S, half)
    cos_t = jnp.tile(cos, (1, LANES // half))
    sin_t = jnp.tile(jnp.concatenate([-sin, sin], axis=1), (1, LANES // QK_ROPE))
    return cos_t, sin_t


def _mla_proj_kernel(scale, h_ref, gn_ref, wd_ref, gq_ref, gkv_ref, wqn_ref, wqr_ref, wqs_ref, wkn_ref, wv_ref,
                     cos_ref, sin_ref, qn_out, qr_out, kn_out, kr_out, v_out):
    u = _rms(h_ref[...], gn_ref[...]).astype(BF16)
    c = _dot(u, wd_ref[...])
    cq = _rms(c[:, :Q_LORA], gq_ref[...]).astype(BF16)
    ckv = _rms(c[:, Q_LORA:Q_LORA + KV_LORA], gkv_ref[...]).astype(BF16)
    o = Q_LORA + KV_LORA
    cos = cos_ref[...]
    sin = sin_ref[...]
    kr_out[...] = (c[:, o:o + LANES] * cos + c[:, o + LANES:o + 2 * LANES] * sin).astype(BF16)
    reps = qr_out.shape[-1] // LANES
    cos_w = jnp.concatenate([cos] * reps, axis=-1)
    sin_w = jnp.concatenate([sin] * reps, axis=-1)
    qn_out[...] = (_dot(cq, wqn_ref[...]) * scale).astype(BF16)
    qr = _dot(cq, wqr_ref[...]) * cos_w + _dot(cq, wqs_ref[...]) * sin_w
    qr_out[...] = (qr * scale).astype(BF16)
    kn_out[...] = _dot(ckv, wkn_ref[...]).astype(BF16)
    v_out[...] = _dot(ckv, wv_ref[...]).astype(BF16)


def _swap_halves(w):
    half = QK_ROPE // 2
    return jnp.concatenate([w[..., half:], w[..., :half]], axis=-1)


def _mla_proj(h2, gn, wd, gq, wuq, gkv, wukv, cos_t, sin_t, tm):
    T, D = h2.shape
    H = MLA_HEADS
    npair = H // 2
    HD = H * QK_NOPE
    kr_w = wd[:, Q_LORA + KV_LORA:]
    zpad = jnp.zeros((D, LANES - 2 * QK_ROPE), F32)
    wd2 = jnp.concatenate([wd[:, :Q_LORA + KV_LORA], kr_w, kr_w, zpad,
                           _swap_halves(kr_w), _swap_halves(kr_w), zpad], axis=1).astype(BF16)
    wq = wuq.reshape(Q_LORA, H, QK_NOPE + QK_ROPE)
    wqn = wq[:, :, :QK_NOPE].reshape(Q_LORA, HD).astype(BF16)
    wq_rope = wq[:, :, QK_NOPE:]

    def pair_pad(w):
        w = w.reshape(Q_LORA, npair, 2 * QK_ROPE)
        w = jnp.concatenate([w, jnp.zeros((Q_LORA, npair, LANES - 2 * QK_ROPE), F32)], axis=-1)
        return w.reshape(Q_LORA, npair * LANES).astype(BF16)

    wqr = pair_pad(wq_rope)
    wqs = pair_pad(_swap_halves(wq_rope))
    wkv = wukv.reshape(KV_LORA, H, QK_NOPE + V_DIM)
    wkn = wkv[:, :, :QK_NOPE].reshape(KV_LORA, HD).astype(BF16)
    wv = wkv[:, :, QK_NOPE:].reshape(KV_LORA, H * V_DIM).astype(BF16)
    scale = math.log2(math.e) / math.sqrt(QK_NOPE + QK_ROPE)
    ws = [wd2, gq.reshape(1, -1), gkv.reshape(1, -1), wqn, wqr, wqs, wkn, wv]
    tokw = lambda n: pl.BlockSpec((tm, n), lambda i: (i, 0))
    return pl.pallas_call(
        functools.partial(_mla_proj_kernel, scale),
        grid=(T // tm,),
        in_specs=[tokw(D), _full((1, D))] + [_full(w.shape) for w in ws] + [tokw(LANES), tokw(LANES)],
        out_specs=[tokw(HD), tokw(npair * LANES), tokw(HD), tokw(LANES), tokw(H * V_DIM)],
        out_shape=[jax.ShapeDtypeStruct((T, HD), BF16), jax.ShapeDtypeStruct((T, npair * LANES), BF16),
                   jax.ShapeDtypeStruct((T, HD), BF16), jax.ShapeDtypeStruct((T, LANES), BF16),
                   jax.ShapeDtypeStruct((T, H * V_DIM), BF16)],
        compiler_params=_cparams("parallel"),
        name="mla_proj",
    )(h2, gn.reshape(1, D), *ws, cos_t, sin_t)


def _attn_kernel(tq, qn_ref, qr_ref, kn_ref, kr_ref, v_ref, o_ref):
    S = qn_ref.shape[1]
    lane = lax.broadcasted_iota(jnp.int32, (1, 2 * LANES), 1)
    head0 = (lane < QK_NOPE) | ((lane >= LANES) & (lane < LANES + QK_ROPE))
    head1 = ((lane >= QK_NOPE) & (lane < LANES)) | ((lane >= LANES + QK_ROPE) & (lane < LANES + 2 * QK_ROPE))
    row = lax.broadcasted_iota(jnp.int32, (2 * tq, tq), 0) % tq
    col = lax.broadcasted_iota(jnp.int32, (2 * tq, tq), 1)
    causal = col <= row
    out_lane = lax.broadcasted_iota(jnp.int32, (1, LANES), 1)

    def scores(i):
        q0, n = i * tq, (i + 1) * tq
        qc = jnp.concatenate([qn_ref[0, q0:n, :], qr_ref[0, q0:n, :]], axis=-1)
        zero = jnp.zeros_like(qc)
        qs = jnp.concatenate([jnp.where(head0, qc, zero), jnp.where(head1, qc, zero)], axis=0)
        kc = jnp.concatenate([kn_ref[0, :n, :], kr_ref[0, :n, :]], axis=-1)
        return lax.dot_general(qs, kc, _NT, preferred_element_type=F32)

    nq = S // tq
    s_next = scores(0)
    for i in range(nq):
        q0, n = i * tq, (i + 1) * tq
        s = s_next
        if i + 1 < nq:
            s_next = scores(i + 1)
        s_diag = jnp.where(causal, s[:, q0:], -1e30)
        s = jnp.concatenate([s[:, :q0], s_diag], axis=-1) if i else s_diag
        m = jnp.max(s, axis=-1, keepdims=True)
        p = jnp.exp2(s - m)
        l = jnp.sum(p, axis=-1, keepdims=True)
        o = _dot(p.astype(BF16), v_ref[0, :n, :]) / l
        o_ref[0, q0:n, :] = jnp.where(out_lane < V_DIM, o[:tq], o[tq:]).astype(o_ref.dtype)


def _attention(qn, qr, kn, kr, v, tq):
    B, S, HD = qn.shape
    npair = HD // LANES
    spec = pl.BlockSpec((1, S, LANES), lambda b, p: (b, 0, p))
    return pl.pallas_call(
        functools.partial(_attn_kernel, tq),
        grid=(B, npair),
        in_specs=[spec, spec, spec, pl.BlockSpec((1, S, LANES), lambda b, p: (b, 0, 0)), spec],
        out_specs=spec,
        out_shape=jax.ShapeDtypeStruct((B, S, HD), BF16),
        compiler_params=_cparams("parallel", "parallel"),
        name="mla_attn",
    )(qn, qr, kn, kr, v)


def _mla_layer(h, positions, gn, wd, gq, wuq, gkv, wukv, tm, tq):
    B, S, D = h.shape
    cos_t, sin_t = _rope_tables(positions)
    qn, qr, kn, kr, v = _mla_proj(h.reshape(B * S, D), gn, wd, gq, wuq, gkv, wukv, cos_t, sin_t, tm)
    sh = lambda t: t.reshape(B, S, t.shape[-1])
    return _attention(sh(qn), sh(qr), sh(kn), sh(kr), sh(v), tq)


def _tile(n, pref):
    t = min(n, pref)
    while n % t:
        t //= 2
    return t


def kernel(x, positions, norm_mix, norm_ffn, norm_final, rw_mu, rw_wr, rw_wk, rw_wv, rw_wo, rw_w0, rw_w1, rw_w2, rw_a0, rw_a1, rw_a2, rw_g1, rw_g2, rw_kk, rw_ka, rw_rk, rw_lnx_w, rw_lnx_b, rw_v0, rw_v1, rw_v2, cv_w1, cv_b1, cv_dw, cv_bdw, cv_ln_w, cv_ln_b, cv_w2, cv_b2, ml_wd, ml_gq, ml_wuq, ml_gkv, ml_wukv, ml_wo, ff_w1, ff_w2):
    B, S, D = x.shape
    depth = norm_mix.shape[0]
    tm = _tile(S, 512)
    tq = _tile(S, 256)
    tf = _tile(ff_w1.shape[-1], 1024)
    h = x
    v_first = None
    zero_bias = jnp.zeros((D,), F32)
    ff_w1_bf = ff_w1.astype(BF16)
    ff_w2_bf = ff_w2.astype(BF16)
    ia = ib = ic = 0
    for i in range(depth):
        kind = i % 3
        if kind == 0:
            vres = None if ia == 0 else (rw_v0[ia - 1], rw_v1[ia - 1], rw_v2[ia - 1], v_first)
            y, v = _rwkv_layer(h, norm_mix[i], rw_mu[ia], rw_wr[ia], rw_wk[ia], rw_wv[ia],
                               rw_w0[ia], rw_w1[ia], rw_w2[ia], rw_a0[ia], rw_a1[ia], rw_a2[ia],
                               rw_g1[ia], rw_g2[ia], rw_kk[ia], rw_ka[ia], rw_rk[ia],
                               rw_lnx_w[ia], rw_lnx_b[ia], vres, tm)
            wo, bo = rw_wo[ia], zero_bias
            if ia == 0:
                v_first = v
            ia += 1
        elif kind == 1:
            y = _conv_layer(h, norm_mix[i], cv_w1[ib], cv_b1[ib], cv_dw[ib], cv_bdw[ib],
                            cv_ln_w[ib], cv_ln_b[ib], tm)
            wo, bo = cv_w2[ib], cv_b2[ib]
            ib += 1
        else:
            y = _mla_layer(h, positions, norm_mix[i], ml_wd[ic], ml_gq[ic], ml_wuq[ic],
                           ml_gkv[ic], ml_wukv[ic], tm, tq)
            wo, bo = ml_wo[ic], zero_bias
            ic += 1
        h = _mlp(h.reshape(B * S, D), y.reshape(B * S, -1), wo.astype(BF16), bo, norm_ffn[i],
                 ff_w1_bf, ff_w2_bf, i, norm_final, i == depth - 1, tm, tf).reshape(B, S, D)
    return h
```

```python
import functools
import math

import jax
import jax.numpy as jnp
from jax import lax
from jax.experimental import pallas as pl
from jax.experimental.pallas import tpu as pltpu

F32 = jnp.float32
BF16 = jnp.bfloat16

NORM_EPS = 1e-5
GN_EPS = 64e-5
RWKV_HEAD = 64
CONV_WIDTH = 31
MLA_HEADS = 16
QK_NOPE = 64
QK_ROPE = 32
V_DIM = 64
Q_LORA = 384
KV_LORA = 256
ROPE_THETA = 10000.0

LANES = 128
SUBLANES = 8
MXU_COLS = 256
WKV_CHUNK = 64
WKV_ROWS = 8
RWKV_PROJ_SPLIT = 1
PROJ_ROWS = 1024
ATTN_AHEAD = 1
VMEM_LIMIT_BYTES = 56 * 1024 * 1024

_NT = (((1,), (1,)), ((), ()))
_TN = (((0,), (0,)), ((), ()))


def _cparams(*sem):
    return pltpu.CompilerParams(dimension_semantics=sem, vmem_limit_bytes=VMEM_LIMIT_BYTES)


def _dot(a, b):
    return jnp.dot(a, b, preferred_element_type=F32)


def _rms(x, g):
    ms = jnp.mean(x * x, axis=-1, keepdims=True)
    return x * lax.rsqrt(ms + NORM_EPS) * g


def _sigmoid(x):
    return 1.0 / (1.0 + jnp.exp(-x))


def _softplus(x):
    return jnp.maximum(x, 0.0) + jnp.log(1.0 + jnp.exp(-jnp.abs(x)))


def _head_sum(x, bd):
    outs = []
    for c in range(x.shape[-1] // MXU_COLS):
        xc = x[:, c * MXU_COLS:(c + 1) * MXU_COLS]
        hi = xc.astype(BF16)
        lo = (xc - hi.astype(F32)).astype(BF16)
        outs.append(_dot(hi, bd) + _dot(lo, bd))
    return jnp.concatenate(outs, axis=-1)


def _row_halves(n):
    return [slice(0, n // 2), slice(n // 2, n)]


def _full(shape):
    n = len(shape)
    return pl.BlockSpec(shape, lambda *_: (0,) * n)


def _mlp_kernel(final_norm, tf, h_ref, y_ref, wo_ref, bo_ref, g_ref, w1_ref, w2_ref, gf_ref, o_ref, a_scr):
    h = h_ref[...] + bo_ref[...] + _dot(y_ref[...], wo_ref[...])
    u = _rms(h, g_ref[...]).astype(BF16)
    for c in range(a_scr.shape[1] // tf):
        cols = slice(c * tf, (c + 1) * tf)
        a = _dot(u, w1_ref[:, cols])
        a_scr[:, cols] = jnp.square(jnp.maximum(a, 0.0)).astype(BF16)
    out = h + _dot(a_scr[...], w2_ref[...])
    if final_norm:
        out = _rms(out, gf_ref[...])
    o_ref[...] = out


def _resident(shape):
    n = len(shape)
    return pl.BlockSpec(shape, lambda *_: (0,) * n, pipeline_mode=pl.Buffered(1))


def _mlp(h2, y2, wo, bo, g, w1_all, w2_all, layer, gf, final_norm, tm, tf):
    T, D = h2.shape
    K = y2.shape[1]
    F = w1_all.shape[2]
    one = pl.Buffered(1)
    return pl.pallas_call(
        functools.partial(_mlp_kernel, final_norm, tf),
        grid=(T // tm,),
        in_specs=[
            pl.BlockSpec((tm, D), lambda i: (i, 0)),
            pl.BlockSpec((tm, K), lambda i: (i, 0)),
            _resident((K, D)),
            _full((1, D)),
            _full((1, D)),
            pl.BlockSpec((None, D, F), lambda i: (layer, 0, 0), pipeline_mode=one),
            pl.BlockSpec((None, F, D), lambda i: (layer, 0, 0), pipeline_mode=one),
            _full((1, D)),
        ],
        out_specs=pl.BlockSpec((tm, D), lambda i: (i, 0)),
        out_shape=jax.ShapeDtypeStruct((T, D), F32),
        scratch_shapes=[pltpu.VMEM((tm, F), BF16)],
        compiler_params=_cparams("parallel"),
        name="mlp",
    )(h2, y2, wo, bo.reshape(1, D), g.reshape(1, D), w1_all, w2_all, gf.reshape(1, D))


def _rwkv_proj_kernel(has_vres, tm, *refs):
    if has_vres:
        (h_ref, hp_ref, gn_ref, mu_ref, vec_ref, wr_ref, wk_ref, wv_ref, w1_ref, w2_ref, a1_ref, a2_ref,
         g1_ref, g2_ref, bd_ref, v1_ref, v2_ref, vf_ref,
         r_out, lw_out, k_out, v_out, a_out, b_out, g_out) = refs
    else:
        (h_ref, hp_ref, gn_ref, mu_ref, vec_ref, wr_ref, wk_ref, wv_ref, w1_ref, w2_ref, a1_ref, a2_ref,
         g1_ref, g2_ref, bd_ref,
         r_out, lw_out, k_out, v_out, a_out, b_out, g_out) = refs
    i = pl.program_id(1)
    gn = gn_ref[...]
    mu = mu_ref[...]
    vec = vec_ref[...]
    w0, a0, k_k, k_a = vec[0:1], vec[1:2], vec[2:3], vec[3:4]
    prev = _rms(hp_ref[0], gn)[SUBLANES - 1:SUBLANES, :]
    prev = jnp.where(i > 0, prev, 0.0)
    nrows = tm // RWKV_PROJ_SPLIT
    row = lax.broadcasted_iota(jnp.int32, (nrows, 1), 0)
    for s in range(RWKV_PROJ_SPLIT):
        rows = slice(s * nrows, (s + 1) * nrows)
        u = _rms(h_ref[0, rows, :], gn)
        shifted = jnp.where(row == 0, prev, pltpu.roll(u, 1, axis=0))
        prev = u[nrows - 1:nrows, :]
        xx = shifted - u

        def mix(j):
            return (u + xx * mu[j:j + 1, :]).astype(BF16)

        xr, xw, xk, xv, xa, xg = [mix(j) for j in range(6)]
        lo_w = _dot(xw, w1_ref[...])
        lo_a = _dot(xa, a1_ref[...])
        lo_g = _dot(xg, g1_ref[...])
        if has_vres:
            lo_v = _dot(xv, v1_ref[...])
        wl = w0 + _dot(jnp.tanh(lo_w).astype(BF16), w2_ref[...])
        a_pre = a0 + _dot(lo_a.astype(BF16), a2_ref[...])
        g = _dot(_sigmoid(lo_g).astype(BF16), g2_ref[...])
        if has_vres:
            gate_pre = vec[4:5] + _dot(lo_v.astype(BF16), v2_ref[...])
        k = _dot(xk, wk_ref[...])
        v = _dot(xv, wv_ref[...])
        r = _dot(xr, wr_ref[...])
        w = -_softplus(-wl) - 0.5
        a = _sigmoid(a_pre)
        if has_vres:
            v = v + (vf_ref[0, rows, :] - v) * _sigmoid(gate_pre)
        kk = k * k_k
        ss = _head_sum(kk * kk, bd_ref[...])
        kk = kk * lax.rsqrt(jnp.maximum(ss, 1e-24))
        r_out[0, rows, :] = r
        lw_out[0, rows, :] = -jnp.exp(w)
        k_out[0, rows, :] = k * (1.0 + (a - 1.0) * k_a)
        v_out[0, rows, :] = v
        a_out[0, rows, :] = -kk
        b_out[0, rows, :] = kk * a
        g_out[0, rows, :] = g


def _block_diag_ones(n, blk):
    idx = jnp.arange(n) // blk
    return (idx[:, None] == idx[None, :]).astype(BF16)


def _rwkv_proj(h, gn, mu, vec, ws, vres, tm):
    B, S, D = h.shape
    has_vres = vres is not None
    tok = pl.BlockSpec((1, tm, D), lambda b, i: (b, i, 0))
    prev = pl.BlockSpec((1, SUBLANES, D), lambda b, i: (b, jnp.maximum(i * (tm // SUBLANES) - 1, 0), 0))
    args = [h, h, gn.reshape(1, D), mu, vec] + list(ws) + [_block_diag_ones(MXU_COLS, RWKV_HEAD)]
    specs = [tok, prev, _full((1, D)), _full(mu.shape), _full(vec.shape)] + [_full(w.shape) for w in ws]
    specs.append(_full((MXU_COLS, MXU_COLS)))
    if has_vres:
        v1, v2, v_first = vres
        args += [v1, v2, v_first]
        specs += [_full(v1.shape), _full(v2.shape), tok]
    out = jax.ShapeDtypeStruct((B, S, D), F32)
    return pl.pallas_call(
        functools.partial(_rwkv_proj_kernel, has_vres, tm),
        grid=(B, S // tm),
        in_specs=specs,
        out_specs=[tok] * 7,
        out_shape=[out] * 7,
        compiler_params=_cparams("parallel", "parallel"),
        name="rwkv_proj",
    )(*args)


def _wkv_kernel(r_ref, lw_ref, k_ref, v_ref, a_ref, b_ref, g_ref, vec_ref, tri_ref, bd_ref,
                y_ref, h_scr, y_scr):
    L = WKV_CHUNK
    nb, _, D = r_ref.shape
    npair = D // LANES

    @pl.when(pl.program_id(1) == 0)
    def _():
        h_scr[...] = jnp.zeros_like(h_scr)

    lane = lax.broadcasted_iota(jnp.int32, (1, LANES), 1)
    first = lane < RWKV_HEAD
    t_idx = lax.broadcasted_iota(jnp.int32, (L, LANES), 0)
    s_idx = lax.broadcasted_iota(jnp.int32, (L, LANES), 1) % RWKV_HEAD
    lower = t_idx > s_idx
    lower_incl = t_idx >= s_idx
    eye = (t_idx == s_idx).astype(F32)
    same_head = (lax.broadcasted_iota(jnp.int32, (LANES, LANES), 0) // RWKV_HEAD
                 == lax.broadcasted_iota(jnp.int32, (LANES, LANES), 1) // RWKV_HEAD)

    def stack(x):
        return jnp.concatenate([jnp.where(first, x, 0.0), jnp.where(first, 0.0, x)], axis=0).astype(BF16)

    ar, bk, blkl, v_b, v_s, h0, w_end = [], [], [], [], [], [], []
    for bi in range(nb):
        lw = lw_ref[bi]
        hi = lw.astype(BF16)
        rest = lw - hi.astype(F32)
        mid = rest.astype(BF16)
        low = (rest - mid.astype(F32)).astype(BF16)
        tri = tri_ref[...]
        cum = _dot(tri, hi) + _dot(tri, mid) + _dot(tri, low)
        e_pos = jnp.exp(cum)
        e_neg = jnp.exp(-cum)
        rt = r_ref[bi] * e_pos
        at = a_ref[bi] * jnp.exp(cum - lw)
        kt = k_ref[bi] * e_neg
        bt = b_ref[bi] * e_neg
        we = e_pos[L - 1:L, :]
        kl = kt * we
        bl = bt * we
        v = v_ref[bi]
        for p in range(npair):
            sl = slice(p * LANES, (p + 1) * LANES)
            ar.append(jnp.concatenate([at[:, sl], rt[:, sl]], axis=0).astype(BF16))
            bk.append(jnp.concatenate([stack(bt[:, sl]), stack(kt[:, sl])], axis=0))
            blkl.append(jnp.concatenate([bl[:, sl], kl[:, sl]], axis=0).astype(BF16))
            v_b.append(v[:, sl].astype(BF16))
            v_s.append(stack(v[:, sl]))
            h0.append(h_scr[bi * npair + p])
            w_end.append(we[:, sl])

    sc = [lax.dot_general(x, y, _NT, preferred_element_type=F32) for x, y in zip(ar, bk)]
    m_ab = [jnp.where(lower, s[:L, :LANES], 0.0) for s in sc]
    m_rb = [jnp.where(lower_incl, s[L:, :LANES], 0.0).astype(BF16) for s in sc]
    m_k = [jnp.concatenate([jnp.where(lower, s[:L, LANES:], 0.0), jnp.where(lower_incl, s[L:, LANES:], 0.0)],
                           axis=0).astype(BF16) for s in sc]
    xy = [_dot(jnp.concatenate([x, m], axis=1), jnp.concatenate([h.astype(BF16), vs], axis=0))
          for x, m, h, vs in zip(ar, m_k, h0, v_s)]
    t_inv = [eye + m for m in m_ab]
    a_cur = [_dot(m.astype(BF16), stack(m)) for m in m_ab]
    n = 4
    while n < L:
        res = [_dot(m.astype(BF16), jnp.concatenate([stack(m), stack(t)], axis=1)) for m, t in zip(a_cur, t_inv)]
        t_inv = [t + x[:, LANES:] for t, x in zip(t_inv, res)]
        a_cur = [x[:, :LANES] for x in res]
        n *= 2
    t_inv = [t + _dot(m.astype(BF16), stack(t)) for t, m in zip(t_inv, a_cur)]
    u = [_dot(t.astype(BF16), stack(x[:L])) for t, x in zip(t_inv, xy)]
    ys = [x[L:] + _dot(m, stack(uu)) for x, m, uu in zip(xy, m_rb, u)]
    dh = [lax.dot_general(w, jnp.concatenate([uu.astype(BF16), vb], axis=0), _TN, preferred_element_type=F32)
          for w, uu, vb in zip(blkl, u, v_b)]
    for j in range(nb * npair):
        bi, p = divmod(j, npair)
        y_scr[bi, :, p * LANES:(p + 1) * LANES] = ys[j]
        decay = jnp.transpose(jnp.broadcast_to(w_end[j], (LANES, LANES)))
        h_scr[j] = h0[j] * decay + jnp.where(same_head, dh[j], 0.0)

    vec = vec_ref[...]
    r_k, ln_w, ln_b = vec[0:1], vec[1:2], vec[2:3]
    bd = bd_ref[...]
    inv_n = 1.0 / RWKV_HEAD
    for bi in range(nb):
        y = y_scr[bi]
        d = y - _head_sum(y, bd) * inv_n
        var = _head_sum(d * d, bd) * inv_n
        yn = d * lax.rsqrt(var + GN_EPS) * ln_w + ln_b
        bonus = _head_sum(r_ref[bi] * k_ref[bi] * r_k, bd) * v_ref[bi]
        y_ref[bi] = ((yn + bonus) * g_ref[bi]).astype(y_ref.dtype)


def _wkv(r, lw, k, v, a, b, g, vec, nb):
    B, S, D = r.shape
    L = WKV_CHUNK
    tok = pl.BlockSpec((nb, L, D), lambda bi, c: (bi, c, 0))
    tri = (jnp.arange(L)[:, None] >= jnp.arange(L)[None, :]).astype(BF16)
    return pl.pallas_call(
        _wkv_kernel,
        grid=(B // nb, S // L),
        in_specs=[tok] * 7 + [_full(vec.shape), _full((L, L)), _full((MXU_COLS, MXU_COLS))],
        out_specs=tok,
        out_shape=jax.ShapeDtypeStruct((B, S, D), BF16),
        scratch_shapes=[pltpu.VMEM((nb * D // LANES, LANES, LANES), F32), pltpu.VMEM((nb, L, D), F32)],
        compiler_params=_cparams("parallel", "arbitrary"),
        name="wkv",
    )(r, lw, k, v, a, b, g, vec, tri, _block_diag_ones(MXU_COLS, RWKV_HEAD))


def _rwkv_layer(h, gn, mu, wr, wk, wv, w0, w1, w2, a0, a1, a2, g1, g2, k_k, k_a, r_k, lnx_w, lnx_b,
                vres, tm):
    B, S, D = h.shape
    rows = [w0, a0, k_k, k_a]
    extra = None
    if vres is not None:
        v0, v1, v2, v_first = vres
        rows.append(v0)
        extra = (v1.astype(BF16), v2.astype(BF16), v_first)
    vec = jnp.stack(rows + [jnp.zeros_like(w0)] * (SUBLANES - len(rows)))
    mu8 = jnp.concatenate([mu, jnp.zeros((SUBLANES - mu.shape[0], D), F32)])
    ws = [w.astype(BF16) for w in (wr, wk, wv, w1, w2, a1, a2, g1, g2)]
    r, lw, k, v, a, b, g = _rwkv_proj(h, gn, mu8, vec, ws, extra, tm)
    vec2 = jnp.stack([r_k.reshape(D), lnx_w, lnx_b] + [jnp.zeros_like(w0)] * (SUBLANES - 3))
    return _wkv(r, lw, k, v, a, b, g, vec2, _tile(B, WKV_ROWS)), v


def _glu_kernel(h_ref, g_ref, w_ref, b_ref, z_ref):
    C = z_ref.shape[-1]
    halves = _row_halves(h_ref.shape[0])
    us = [_rms(h_ref[r, :], g_ref[...]).astype(BF16) for r in halves]
    zs = [_dot(u, w_ref[...]) for u in us]
    for r, z in zip(halves, zs):
        z = z + b_ref[...]
        z_ref[r, :] = z[:, :C] * _sigmoid(z[:, C:])


def _glu(h2, g, w1, b1, tm):
    T, D = h2.shape
    N = w1.shape[1]
    return pl.pallas_call(
        _glu_kernel,
        grid=(T // tm,),
        in_specs=[pl.BlockSpec((tm, D), lambda i: (i, 0)), _full((1, D)), _full((D, N)), _full((1, N))],
        out_specs=pl.BlockSpec((tm, N // 2), lambda i: (i, 0)),
        out_shape=jax.ShapeDtypeStruct((T, N // 2), F32),
        compiler_params=_cparams("parallel"),
        name="conv_glu",
    )(h2, g.reshape(1, D), w1, b1.reshape(1, N))


_CONV_HALO = 32


_CONV_ROWS = 64
_CONV_COLS = 128


def _dwconv_kernel(tm, z_ref, zp_ref, dw_ref, vec_ref, o_ref, zbuf):
    i = pl.program_id(1)
    C = z_ref.shape[-1]
    zbuf[0:_CONV_HALO, :] = jnp.where(i > 0, zp_ref[0], 0.0)
    zbuf[_CONV_HALO:, :] = z_ref[0]
    vec = vec_ref[...]
    bdw, ln_w, ln_b = vec[0:1], vec[1:2], vec[2:3]
    off = _CONV_HALO - (CONV_WIDTH - 1)
    span = _CONV_ROWS + _CONV_HALO
    for r0 in range(0, tm, _CONV_ROWS):
        parts = []
        for c0 in range(0, C, _CONV_COLS):
            cols = slice(c0, c0 + _CONV_COLS)
            acc = None
            win = zbuf[r0:r0 + span, cols]
            for res in range(SUBLANES):
                shifted = pltpu.roll(win, span - res, axis=0) if res else win
                for o in range(off + (res - off) % SUBLANES, off + CONV_WIDTH, SUBLANES):
                    q = o - res
                    term = shifted[q:q + _CONV_ROWS, :] * dw_ref[o - off:o - off + 1, cols]
                    acc = term if acc is None else acc + term
            parts.append(acc)
        acc = jnp.concatenate(parts, axis=-1) + bdw
        mean = jnp.mean(acc, axis=-1, keepdims=True)
        d = acc - mean
        var = jnp.mean(d * d, axis=-1, keepdims=True)
        y = d * lax.rsqrt(var + NORM_EPS) * ln_w + ln_b
        o_ref[0, r0:r0 + _CONV_ROWS, :] = (y * _sigmoid(y)).astype(o_ref.dtype)


def _dwconv(z, dw, vec, tm):
    B, S, C = z.shape
    tok = pl.BlockSpec((1, tm, C), lambda b, i: (b, i, 0))
    prev = pl.BlockSpec((1, _CONV_HALO, C), lambda b, i: (b, jnp.maximum(i * (tm // _CONV_HALO) - 1, 0), 0))
    return pl.pallas_call(
        functools.partial(_dwconv_kernel, tm),
        grid=(B, S // tm),
        in_specs=[tok, prev, _full(dw.shape), _full(vec.shape)],
        out_specs=tok,
        out_shape=jax.ShapeDtypeStruct((B, S, C), BF16),
        scratch_shapes=[pltpu.VMEM((_CONV_HALO + tm, C), F32)],
        compiler_params=_cparams("parallel", "parallel"),
        name="conv_dw",
    )(z, z, dw, vec)


def _conv_layer(h, gn, w1, b1, dw, bdw, ln_w, ln_b, tm):
    B, S, D = h.shape
    z = _glu(h.reshape(B * S, D), gn, w1.astype(BF16), b1, _tile(B * S, PROJ_ROWS)).reshape(B, S, -1)
    dw32 = jnp.concatenate([dw, jnp.zeros((32 - CONV_WIDTH, dw.shape[1]), F32)])
    vec = jnp.stack([bdw, ln_w, ln_b] + [jnp.zeros_like(bdw)] * (SUBLANES - 3))
    return _dwconv(z, dw32, vec, tm)


def _rope_table_kernel(pos_ref, frq_ref, cos_ref, sin_ref):
    ang = pos_ref[...] * frq_ref[...]
    cos_ref[...] = jnp.cos(ang)
    sin_ref[...] = jnp.sin(ang)


def _rope_tables(positions):
    B, S = positions.shape
    half = QK_ROPE // 2
    per_row = LANES // half
    rows = B * S // per_row
    inv_freq = ROPE_THETA ** (-jnp.arange(0, QK_ROPE, 2, dtype=F32) / QK_ROPE)
    pos = jnp.repeat(positions.astype(F32).reshape(rows, per_row), half, axis=1)
    frq = jnp.tile(inv_freq, per_row).reshape(1, LANES)
    tr = min(rows, 512)
    cos, sin = pl.pallas_call(
        _rope_table_kernel,
        grid=(rows // tr,),
        in_specs=[pl.BlockSpec((tr, LANES), lambda i: (i, 0)), _full((1, LANES))],
        out_specs=[pl.BlockSpec((tr, LANES), lambda i: (i, 0))] * 2,
        out_shape=[jax.ShapeDtypeStruct((rows, LANES), F32)] * 2,
        compiler_params=_cparams("parallel"),
        name="rope_table",
    )(pos, frq)
    cos = cos.reshape(B * S, half)
    sin = sin.reshape(B * S, half)
    cos_t = jnp.tile(cos, (1, LANES // half))
    sin_t = jnp.tile(jnp.concatenate([-sin, sin], axis=1), (1, LANES // QK_ROPE))
    return cos_t, sin_t


def _mla_proj_kernel(scale, h_ref, gn_ref, wd_ref, gq_ref, gkv_ref, wqn_ref, wqr_ref, wqs_ref, wkn_ref, wv_ref,
                     cos_ref, sin_ref, qn_out, qr_out, kn_out, kr_out, v_out):
    halves = _row_halves(h_ref.shape[0])
    us = [_rms(h_ref[r, :], gn_ref[...]).astype(BF16) for r in halves]
    cs = [_dot(u, wd_ref[...]) for u in us]
    o = Q_LORA + KV_LORA
    reps = qr_out.shape[-1] // LANES
    for r, c in zip(halves, cs):
        cq = _rms(c[:, :Q_LORA], gq_ref[...]).astype(BF16)
        ckv = _rms(c[:, Q_LORA:o], gkv_ref[...]).astype(BF16)
        cos = cos_ref[r, :]
        sin = sin_ref[r, :]
        kr_out[r, :] = (c[:, o:o + LANES] * cos + c[:, o + LANES:o + 2 * LANES] * sin).astype(BF16)
        cos_w = jnp.concatenate([cos] * reps, axis=-1)
        sin_w = jnp.concatenate([sin] * reps, axis=-1)
        qn_out[r, :] = (_dot(cq, wqn_ref[...]) * scale).astype(BF16)
        qr = _dot(cq, wqr_ref[...]) * cos_w + _dot(cq, wqs_ref[...]) * sin_w
        qr_out[r, :] = (qr * scale).astype(BF16)
        kn_out[r, :] = _dot(ckv, wkn_ref[...]).astype(BF16)
        v_out[r, :] = _dot(ckv, wv_ref[...]).astype(BF16)


def _swap_halves(w):
    half = QK_ROPE // 2
    return jnp.concatenate([w[..., half:], w[..., :half]], axis=-1)


def _mla_proj(h2, gn, wd, gq, wuq, gkv, wukv, cos_t, sin_t, tm):
    T, D = h2.shape
    H = MLA_HEADS
    npair = H // 2
    HD = H * QK_NOPE
    kr_w = wd[:, Q_LORA + KV_LORA:]
    zpad = jnp.zeros((D, LANES - 2 * QK_ROPE), F32)
    wd2 = jnp.concatenate([wd[:, :Q_LORA + KV_LORA], kr_w, kr_w, zpad,
                           _swap_halves(kr_w), _swap_halves(kr_w), zpad], axis=1).astype(BF16)
    wq = wuq.reshape(Q_LORA, H, QK_NOPE + QK_ROPE)
    wqn = wq[:, :, :QK_NOPE].reshape(Q_LORA, HD).astype(BF16)
    wq_rope = wq[:, :, QK_NOPE:]

    def pair_pad(w):
        w = w.reshape(Q_LORA, npair, 2 * QK_ROPE)
        w = jnp.concatenate([w, jnp.zeros((Q_LORA, npair, LANES - 2 * QK_ROPE), F32)], axis=-1)
        return w.reshape(Q_LORA, npair * LANES).astype(BF16)

    wqr = pair_pad(wq_rope)
    wqs = pair_pad(_swap_halves(wq_rope))
    wkv = wukv.reshape(KV_LORA, H, QK_NOPE + V_DIM)
    wkn = wkv[:, :, :QK_NOPE].reshape(KV_LORA, HD).astype(BF16)
    wv = wkv[:, :, QK_NOPE:].reshape(KV_LORA, H * V_DIM).astype(BF16)
    scale = math.log2(math.e) / math.sqrt(QK_NOPE + QK_ROPE)
    ws = [wd2, gq.reshape(1, -1), gkv.reshape(1, -1), wqn, wqr, wqs, wkn, wv]
    tokw = lambda n: pl.BlockSpec((tm, n), lambda i: (i, 0))
    return pl.pallas_call(
        functools.partial(_mla_proj_kernel, scale),
        grid=(T // tm,),
        in_specs=[tokw(D), _full((1, D))] + [_full(w.shape) for w in ws] + [tokw(LANES), tokw(LANES)],
        out_specs=[tokw(HD), tokw(npair * LANES), tokw(HD), tokw(LANES), tokw(H * V_DIM)],
        out_shape=[jax.ShapeDtypeStruct((T, HD), BF16), jax.ShapeDtypeStruct((T, npair * LANES), BF16),
                   jax.ShapeDtypeStruct((T, HD), BF16), jax.ShapeDtypeStruct((T, LANES), BF16),
                   jax.ShapeDtypeStruct((T, H * V_DIM), BF16)],
        compiler_params=_cparams("parallel"),
        name="mla_proj",
    )(h2, gn.reshape(1, D), *ws, cos_t, sin_t)


def _attn_kernel(tq, qn_ref, qr_ref, kn_ref, kr_ref, v_ref, o_ref):
    S = qn_ref.shape[1]
    lane = lax.broadcasted_iota(jnp.int32, (1, 2 * LANES), 1)
    head0 = (lane < QK_NOPE) | ((lane >= LANES) & (lane < LANES + QK_ROPE))
    head1 = ((lane >= QK_NOPE) & (lane < LANES)) | ((lane >= LANES + QK_ROPE) & (lane < LANES + 2 * QK_ROPE))
    row = lax.broadcasted_iota(jnp.int32, (2 * tq, tq), 0) % tq
    col = lax.broadcasted_iota(jnp.int32, (2 * tq, tq), 1)
    causal = col <= row
    out_lane = lax.broadcasted_iota(jnp.int32, (1, LANES), 1)

    def scores(i):
        q0, n = i * tq, (i + 1) * tq
        qc = jnp.concatenate([qn_ref[0, q0:n, :], qr_ref[0, q0:n, :]], axis=-1)
        zero = jnp.zeros_like(qc)
        qs = jnp.concatenate([jnp.where(head0, qc, zero), jnp.where(head1, qc, zero)], axis=0)
        kc = jnp.concatenate([kn_ref[0, :n, :], kr_ref[0, :n, :]], axis=-1)
        return lax.dot_general(qs, kc, _NT, preferred_element_type=F32)

    nq = S // tq
    ahead = [scores(i) for i in range(min(ATTN_AHEAD, nq))]
    for i in range(nq):
        q0, n = i * tq, (i + 1) * tq
        s = ahead.pop(0)
        if i + ATTN_AHEAD < nq:
            ahead.append(scores(i + ATTN_AHEAD))
        s_diag = jnp.where(causal, s[:, q0:], -1e30)
        s = jnp.concatenate([s[:, :q0], s_diag], axis=-1) if i else s_diag
        m = jnp.max(s, axis=-1, keepdims=True)
        p = jnp.exp2(s - m)
        l = jnp.sum(p, axis=-1, keepdims=True)
        o = _dot(p.astype(BF16), v_ref[0, :n, :]) / l
        o_ref[0, q0:n, :] = jnp.where(out_lane < V_DIM, o[:tq], o[tq:]).astype(o_ref.dtype)


def _attention(qn, qr, kn, kr, v, tq):
    B, S, HD = qn.shape
    npair = HD // LANES
    spec = pl.BlockSpec((1, S, LANES), lambda b, p: (b, 0, p))
    return pl.pallas_call(
        functools.partial(_attn_kernel, tq),
        grid=(B, npair),
        in_specs=[spec, spec, spec, pl.BlockSpec((1, S, LANES), lambda b, p: (b, 0, 0)), spec],
        out_specs=spec,
        out_shape=jax.ShapeDtypeStruct((B, S, HD), BF16),
        compiler_params=_cparams("parallel", "parallel"),
        name="mla_attn",
    )(qn, qr, kn, kr, v)


def _mla_layer(h, positions, gn, wd, gq, wuq, gkv, wukv, tm, tq):
    B, S, D = h.shape
    cos_t, sin_t = _rope_tables(positions)
    qn, qr, kn, kr, v = _mla_proj(h.reshape(B * S, D), gn, wd, gq, wuq, gkv, wukv, cos_t, sin_t,
                                  _tile(B * S, PROJ_ROWS))
    sh = lambda t: t.reshape(B, S, t.shape[-1])
    return _attention(sh(qn), sh(qr), sh(kn), sh(kr), sh(v), tq)


def _tile(n, pref):
    t = min(n, pref)
    while n % t:
        t //= 2
    return t


def kernel(x, positions, norm_mix, norm_ffn, norm_final, rw_mu, rw_wr, rw_wk, rw_wv, rw_wo, rw_w0, rw_w1, rw_w2, rw_a0, rw_a1, rw_a2, rw_g1, rw_g2, rw_kk, rw_ka, rw_rk, rw_lnx_w, rw_lnx_b, rw_v0, rw_v1, rw_v2, cv_w1, cv_b1, cv_dw, cv_bdw, cv_ln_w, cv_ln_b, cv_w2, cv_b2, ml_wd, ml_gq, ml_wuq, ml_gkv, ml_wukv, ml_wo, ff_w1, ff_w2):
    B, S, D = x.shape
    depth = norm_mix.shape[0]
    tm = _tile(S, 512)
    tq = _tile(S, 256)
    tf = _tile(ff_w1.shape[-1], 1024)
    h = x
    v_first = None
    zero_bias = jnp.zeros((D,), F32)
    ff_w1_bf = ff_w1.astype(BF16)
    ff_w2_bf = ff_w2.astype(BF16)
    ia = ib = ic = 0
    for i in range(depth):
        kind = i % 3
        if kind == 0:
            vres = None if ia == 0 else (rw_v0[ia - 1], rw_v1[ia - 1], rw_v2[ia - 1], v_first)
            y, v = _rwkv_layer(h, norm_mix[i], rw_mu[ia], rw_wr[ia], rw_wk[ia], rw_wv[ia],
                               rw_w0[ia], rw_w1[ia], rw_w2[ia], rw_a0[ia], rw_a1[ia], rw_a2[ia],
                               rw_g1[ia], rw_g2[ia], rw_kk[ia], rw_ka[ia], rw_rk[ia],
                               rw_lnx_w[ia], rw_lnx_b[ia], vres, tm)
            wo, bo = rw_wo[ia], zero_bias
            if ia == 0:
                v_first = v
            ia += 1
        elif kind == 1:
            y = _conv_layer(h, norm_mix[i], cv_w1[ib], cv_b1[ib], cv_dw[ib], cv_bdw[ib],
                            cv_ln_w[ib], cv_ln_b[ib], tm)
            wo, bo = cv_w2[ib], cv_b2[ib]
            ib += 1
        else:
            y = _mla_layer(h, positions, norm_mix[i], ml_wd[ic], ml_gq[ic], ml_wuq[ic],
                           ml_gkv[ic], ml_wukv[ic], tm, tq)
            wo, bo = ml_wo[ic], zero_bias
            ic += 1
        h = _mlp(h.reshape(B * S, D), y.reshape(B * S, -1), wo.astype(BF16), bo, norm_ffn[i],
                 ff_w1_bf, ff_w2_bf, i, norm_final, i == depth - 1, tm, tf).reshape(B, S, D)
    return h
```

```python
import functools
import math

import jax
import jax.numpy as jnp
from jax import lax
from jax.experimental import pallas as pl
from jax.experimental.pallas import tpu as pltpu

F32 = jnp.float32
BF16 = jnp.bfloat16

NORM_EPS = 1e-5
GN_EPS = 64e-5
RWKV_HEAD = 64
CONV_WIDTH = 31
MLA_HEADS = 16
QK_NOPE = 64
QK_ROPE = 32
V_DIM = 64
Q_LORA = 384
KV_LORA = 256
ROPE_THETA = 10000.0

LANES = 128
SUBLANES = 8
MXU_COLS = 256
WKV_CHUNK = 64
WKV_ROWS = 8
RWKV_PROJ_SPLIT = 1
MLP_ROWS = 1024
PROJ_ROWS = 1024
ATTN_AHEAD = 1
VMEM_LIMIT_BYTES = 56 * 1024 * 1024

_NT = (((1,), (1,)), ((), ()))
_TN = (((0,), (0,)), ((), ()))


def _cparams(*sem):
    return pltpu.CompilerParams(dimension_semantics=sem, vmem_limit_bytes=VMEM_LIMIT_BYTES)


def _dot(a, b):
    return jnp.dot(a, b, preferred_element_type=F32)


def _rms(x, g):
    ms = jnp.mean(x * x, axis=-1, keepdims=True)
    return x * lax.rsqrt(ms + NORM_EPS) * g


def _sigmoid(x):
    return 1.0 / (1.0 + jnp.exp(-x))


def _softplus(x):
    return jnp.maximum(x, 0.0) + jnp.log(1.0 + jnp.exp(-jnp.abs(x)))


def _head_sum(x, bd):
    outs = []
    for c in range(x.shape[-1] // MXU_COLS):
        outs.append(_dot(x[:, c * MXU_COLS:(c + 1) * MXU_COLS].astype(BF16), bd))
    return jnp.concatenate(outs, axis=-1)


def _row_halves(n):
    return [slice(0, n // 2), slice(n // 2, n)]


def _full(shape):
    n = len(shape)
    return pl.BlockSpec(shape, lambda *_: (0,) * n)


def _mlp_kernel(final_norm, tf, h_ref, y_ref, wo_ref, bo_ref, g_ref, w1_ref, w2_ref, gf_ref, o_ref, a_scr):
    h = h_ref[...] + bo_ref[...] + _dot(y_ref[...], wo_ref[...])
    u = _rms(h, g_ref[...]).astype(BF16)
    for c in range(a_scr.shape[1] // tf):
        cols = slice(c * tf, (c + 1) * tf)
        a = _dot(u, w1_ref[:, cols])
        a_scr[:, cols] = jnp.square(jnp.maximum(a, 0.0)).astype(BF16)
    out = h + _dot(a_scr[...], w2_ref[...])
    if final_norm:
        out = _rms(out, gf_ref[...])
    o_ref[...] = out


def _resident(shape):
    n = len(shape)
    return pl.BlockSpec(shape, lambda *_: (0,) * n, pipeline_mode=pl.Buffered(1))


def _mlp(h2, y2, wo, bo, g, w1_all, w2_all, layer, gf, final_norm, tm, tf):
    T, D = h2.shape
    K = y2.shape[1]
    F = w1_all.shape[2]
    one = pl.Buffered(1)
    return pl.pallas_call(
        functools.partial(_mlp_kernel, final_norm, tf),
        grid=(T // tm,),
        in_specs=[
            pl.BlockSpec((tm, D), lambda i: (i, 0)),
            pl.BlockSpec((tm, K), lambda i: (i, 0)),
            _resident((K, D)),
            _full((1, D)),
            _full((1, D)),
            pl.BlockSpec((None, D, F), lambda i: (layer, 0, 0), pipeline_mode=one),
            pl.BlockSpec((None, F, D), lambda i: (layer, 0, 0), pipeline_mode=one),
            _full((1, D)),
        ],
        out_specs=pl.BlockSpec((tm, D), lambda i: (i, 0)),
        out_shape=jax.ShapeDtypeStruct((T, D), F32),
        scratch_shapes=[pltpu.VMEM((tm, F), BF16)],
        compiler_params=_cparams("parallel"),
        name="mlp",
    )(h2, y2, wo, bo.reshape(1, D), g.reshape(1, D), w1_all, w2_all, gf.reshape(1, D))


def _rwkv_proj_kernel(has_vres, tm, *refs):
    if has_vres:
        (h_ref, hp_ref, gn_ref, mu_ref, vec_ref, wr_ref, wk_ref, wv_ref, w1_ref, w2_ref, a1_ref, a2_ref,
         g1_ref, g2_ref, bd_ref, v1_ref, v2_ref, vf_ref,
         r_out, lw_out, k_out, v_out, a_out, b_out, g_out) = refs
    else:
        (h_ref, hp_ref, gn_ref, mu_ref, vec_ref, wr_ref, wk_ref, wv_ref, w1_ref, w2_ref, a1_ref, a2_ref,
         g1_ref, g2_ref, bd_ref,
         r_out, lw_out, k_out, v_out, a_out, b_out, g_out) = refs
    i = pl.program_id(1)
    gn = gn_ref[...]
    mu = mu_ref[...]
    vec = vec_ref[...]
    w0, a0, k_k, k_a = vec[0:1], vec[1:2], vec[2:3], vec[3:4]
    prev = _rms(hp_ref[0], gn)[SUBLANES - 1:SUBLANES, :]
    prev = jnp.where(i > 0, prev, 0.0)
    nrows = tm // RWKV_PROJ_SPLIT
    row = lax.broadcasted_iota(jnp.int32, (nrows, 1), 0)
    for s in range(RWKV_PROJ_SPLIT):
        rows = slice(s * nrows, (s + 1) * nrows)
        u = _rms(h_ref[0, rows, :], gn)
        shifted = jnp.where(row == 0, prev, pltpu.roll(u, 1, axis=0))
        prev = u[nrows - 1:nrows, :]
        xx = shifted - u

        def mix(j):
            return (u + xx * mu[j:j + 1, :]).astype(BF16)

        xr, xw, xk, xv, xa, xg = [mix(j) for j in range(6)]
        lo_w = _dot(xw, w1_ref[...])
        lo_a = _dot(xa, a1_ref[...])
        lo_g = _dot(xg, g1_ref[...])
        if has_vres:
            lo_v = _dot(xv, v1_ref[...])
        wl = w0 + _dot(jnp.tanh(lo_w).astype(BF16), w2_ref[...])
        a_pre = a0 + _dot(lo_a.astype(BF16), a2_ref[...])
        g = _dot(_sigmoid(lo_g).astype(BF16), g2_ref[...])
        if has_vres:
            gate_pre = vec[4:5] + _dot(lo_v.astype(BF16), v2_ref[...])
        k = _dot(xk, wk_ref[...])
        v = _dot(xv, wv_ref[...])
        r = _dot(xr, wr_ref[...])
        w = -_softplus(-wl) - 0.5
        a = _sigmoid(a_pre)
        if has_vres:
            v = v + (vf_ref[0, rows, :] - v) * _sigmoid(gate_pre)
        kk = k * k_k
        ss = _head_sum(kk * kk, bd_ref[...])
        kk = kk * lax.rsqrt(jnp.maximum(ss, 1e-24))
        r_out[0, rows, :] = r
        lw_out[0, rows, :] = -jnp.exp(w)
        k_out[0, rows, :] = k * (1.0 + (a - 1.0) * k_a)
        v_out[0, rows, :] = v
        a_out[0, rows, :] = -kk
        b_out[0, rows, :] = kk * a
        g_out[0, rows, :] = g


def _block_diag_ones(n, blk):
    idx = jnp.arange(n) // blk
    return (idx[:, None] == idx[None, :]).astype(BF16)


def _rwkv_proj(h, gn, mu, vec, ws, vres, tm):
    B, S, D = h.shape
    has_vres = vres is not None
    tok = pl.BlockSpec((1, tm, D), lambda b, i: (b, i, 0))
    prev = pl.BlockSpec((1, SUBLANES, D), lambda b, i: (b, jnp.maximum(i * (tm // SUBLANES) - 1, 0), 0))
    args = [h, h, gn.reshape(1, D), mu, vec] + list(ws) + [_block_diag_ones(MXU_COLS, RWKV_HEAD)]
    specs = [tok, prev, _full((1, D)), _full(mu.shape), _full(vec.shape)] + [_full(w.shape) for w in ws]
    specs.append(_full((MXU_COLS, MXU_COLS)))
    if has_vres:
        v1, v2, v_first = vres
        args += [v1, v2, v_first]
        specs += [_full(v1.shape), _full(v2.shape), tok]
    out = jax.ShapeDtypeStruct((B, S, D), F32)
    return pl.pallas_call(
        functools.partial(_rwkv_proj_kernel, has_vres, tm),
        grid=(B, S // tm),
        in_specs=specs,
        out_specs=[tok] * 7,
        out_shape=[out] * 7,
        compiler_params=_cparams("parallel", "parallel"),
        name="rwkv_proj",
    )(*args)


def _wkv_kernel(r_ref, lw_ref, k_ref, v_ref, a_ref, b_ref, g_ref, vec_ref, tri_ref, bd_ref,
                y_ref, h_scr, y_scr):
    L = WKV_CHUNK
    nb, _, D = r_ref.shape
    npair = D // LANES

    @pl.when(pl.program_id(1) == 0)
    def _():
        h_scr[...] = jnp.zeros_like(h_scr)

    lane = lax.broadcasted_iota(jnp.int32, (1, LANES), 1)
    first = lane < RWKV_HEAD
    t_idx = lax.broadcasted_iota(jnp.int32, (L, LANES), 0)
    s_idx = lax.broadcasted_iota(jnp.int32, (L, LANES), 1) % RWKV_HEAD
    lower = t_idx > s_idx
    lower_incl = t_idx >= s_idx
    eye = (t_idx == s_idx).astype(F32)
    same_head = (lax.broadcasted_iota(jnp.int32, (LANES, LANES), 0) // RWKV_HEAD
                 == lax.broadcasted_iota(jnp.int32, (LANES, LANES), 1) // RWKV_HEAD)

    def stack(x):
        return jnp.concatenate([jnp.where(first, x, 0.0), jnp.where(first, 0.0, x)], axis=0).astype(BF16)

    ar, bk, blkl, v_b, v_s, h0, w_end = [], [], [], [], [], [], []
    for bi in range(nb):
        lw = lw_ref[bi]
        hi = lw.astype(BF16)
        low = (lw - hi.astype(F32)).astype(BF16)
        tri = tri_ref[...]
        cum = _dot(tri, hi) + _dot(tri, low)
        e_pos = jnp.exp(cum)
        e_neg = jnp.exp(-cum)
        rt = r_ref[bi] * e_pos
        at = a_ref[bi] * jnp.exp(cum - lw)
        kt = k_ref[bi] * e_neg
        bt = b_ref[bi] * e_neg
        we = e_pos[L - 1:L, :]
        kl = kt * we
        bl = bt * we
        v = v_ref[bi]
        for p in range(npair):
            sl = slice(p * LANES, (p + 1) * LANES)
            ar.append(jnp.concatenate([at[:, sl], rt[:, sl]], axis=0).astype(BF16))
            bk.append(jnp.concatenate([stack(bt[:, sl]), stack(kt[:, sl])], axis=0))
            blkl.append(jnp.concatenate([bl[:, sl], kl[:, sl]], axis=0).astype(BF16))
            v_b.append(v[:, sl].astype(BF16))
            v_s.append(stack(v[:, sl]))
            h0.append(h_scr[bi * npair + p])
            w_end.append(we[:, sl])

    sc = [lax.dot_general(x, y, _NT, preferred_element_type=F32) for x, y in zip(ar, bk)]
    m_ab = [jnp.where(lower, s[:L, :LANES], 0.0) for s in sc]
    m_rb = [jnp.where(lower_incl, s[L:, :LANES], 0.0).astype(BF16) for s in sc]
    m_k = [jnp.concatenate([jnp.where(lower, s[:L, LANES:], 0.0), jnp.where(lower_incl, s[L:, LANES:], 0.0)],
                           axis=0).astype(BF16) for s in sc]
    xy = [_dot(jnp.concatenate([x, m], axis=1), jnp.concatenate([h.astype(BF16), vs], axis=0))
          for x, m, h, vs in zip(ar, m_k, h0, v_s)]
    t_inv = [eye + m for m in m_ab]
    a_cur = [_dot(m.astype(BF16), stack(m)) for m in m_ab]
    n = 4
    while n < L:
        res = [_dot(m.astype(BF16), jnp.concatenate([stack(m), stack(t)], axis=1)) for m, t in zip(a_cur, t_inv)]
        t_inv = [t + x[:, LANES:] for t, x in zip(t_inv, res)]
        a_cur = [x[:, :LANES] for x in res]
        n *= 2
    t_inv = [t + _dot(m.astype(BF16), stack(t)) for t, m in zip(t_inv, a_cur)]
    u = [_dot(t.astype(BF16), stack(x[:L])) for t, x in zip(t_inv, xy)]
    ys = [x[L:] + _dot(m, stack(uu)) for x, m, uu in zip(xy, m_rb, u)]
    dh = [lax.dot_general(w, jnp.concatenate([uu.astype(BF16), vb], axis=0), _TN, preferred_element_type=F32)
          for w, uu, vb in zip(blkl, u, v_b)]
    for j in range(nb * npair):
        bi, p = divmod(j, npair)
        y_scr[bi, :, p * LANES:(p + 1) * LANES] = ys[j]
        decay = jnp.transpose(jnp.broadcast_to(w_end[j], (LANES, LANES)))
        h_scr[j] = h0[j] * decay + jnp.where(same_head, dh[j], 0.0)

    vec = vec_ref[...]
    r_k, ln_w, ln_b = vec[0:1], vec[1:2], vec[2:3]
    bd = bd_ref[...]
    inv_n = 1.0 / RWKV_HEAD
    for bi in range(nb):
        y = y_scr[bi]
        d = y - _head_sum(y, bd) * inv_n
        var = _head_sum(d * d, bd) * inv_n
        yn = d * lax.rsqrt(var + GN_EPS) * ln_w + ln_b
        bonus = _head_sum(r_ref[bi] * k_ref[bi] * r_k, bd) * v_ref[bi]
        y_ref[bi] = ((yn + bonus) * g_ref[bi]).astype(y_ref.dtype)


def _wkv(r, lw, k, v, a, b, g, vec, nb):
    B, S, D = r.shape
    L = WKV_CHUNK
    tok = pl.BlockSpec((nb, L, D), lambda bi, c: (bi, c, 0))
    tri = (jnp.arange(L)[:, None] >= jnp.arange(L)[None, :]).astype(BF16)
    return pl.pallas_call(
        _wkv_kernel,
        grid=(B // nb, S // L),
        in_specs=[tok] * 7 + [_full(vec.shape), _full((L, L)), _full((MXU_COLS, MXU_COLS))],
        out_specs=tok,
        out_shape=jax.ShapeDtypeStruct((B, S, D), BF16),
        scratch_shapes=[pltpu.VMEM((nb * D // LANES, LANES, LANES), F32), pltpu.VMEM((nb, L, D), F32)],
        compiler_params=_cparams("parallel", "arbitrary"),
        name="wkv",
    )(r, lw, k, v, a, b, g, vec, tri, _block_diag_ones(MXU_COLS, RWKV_HEAD))


def _rwkv_layer(h, gn, mu, wr, wk, wv, w0, w1, w2, a0, a1, a2, g1, g2, k_k, k_a, r_k, lnx_w, lnx_b,
                vres, tm):
    B, S, D = h.shape
    rows = [w0, a0, k_k, k_a]
    extra = None
    if vres is not None:
        v0, v1, v2, v_first = vres
        rows.append(v0)
        extra = (v1.astype(BF16), v2.astype(BF16), v_first)
    vec = jnp.stack(rows + [jnp.zeros_like(w0)] * (SUBLANES - len(rows)))
    mu8 = jnp.concatenate([mu, jnp.zeros((SUBLANES - mu.shape[0], D), F32)])
    ws = [w.astype(BF16) for w in (wr, wk, wv, w1, w2, a1, a2, g1, g2)]
    r, lw, k, v, a, b, g = _rwkv_proj(h, gn, mu8, vec, ws, extra, tm)
    vec2 = jnp.stack([r_k.reshape(D), lnx_w, lnx_b] + [jnp.zeros_like(w0)] * (SUBLANES - 3))
    return _wkv(r, lw, k, v, a, b, g, vec2, _tile(B, WKV_ROWS)), v


def _glu_kernel(h_ref, g_ref, w_ref, b_ref, z_ref):
    C = z_ref.shape[-1]
    halves = _row_halves(h_ref.shape[0])
    us = [_rms(h_ref[r, :], g_ref[...]).astype(BF16) for r in halves]
    zs = [_dot(u, w_ref[...]) for u in us]
    for r, z in zip(halves, zs):
        z = z + b_ref[...]
        z_ref[r, :] = z[:, :C] * _sigmoid(z[:, C:])


def _glu(h2, g, w1, b1, tm):
    T, D = h2.shape
    N = w1.shape[1]
    return pl.pallas_call(
        _glu_kernel,
        grid=(T // tm,),
        in_specs=[pl.BlockSpec((tm, D), lambda i: (i, 0)), _full((1, D)), _full((D, N)), _full((1, N))],
        out_specs=pl.BlockSpec((tm, N // 2), lambda i: (i, 0)),
        out_shape=jax.ShapeDtypeStruct((T, N // 2), F32),
        compiler_params=_cparams("parallel"),
        name="conv_glu",
    )(h2, g.reshape(1, D), w1, b1.reshape(1, N))


_CONV_HALO = 32


_CONV_ROWS = 64
_CONV_COLS = 128


def _dwconv_kernel(tm, z_ref, zp_ref, dw_ref, vec_ref, o_ref, zbuf):
    i = pl.program_id(1)
    C = z_ref.shape[-1]
    zbuf[0:_CONV_HALO, :] = jnp.where(i > 0, zp_ref[0], 0.0)
    zbuf[_CONV_HALO:, :] = z_ref[0]
    vec = vec_ref[...]
    bdw, ln_w, ln_b = vec[0:1], vec[1:2], vec[2:3]
    off = _CONV_HALO - (CONV_WIDTH - 1)
    span = _CONV_ROWS + _CONV_HALO
    for r0 in range(0, tm, _CONV_ROWS):
        parts = []
        for c0 in range(0, C, _CONV_COLS):
            cols = slice(c0, c0 + _CONV_COLS)
            acc = None
            win = zbuf[r0:r0 + span, cols]
            for res in range(SUBLANES):
                shifted = pltpu.roll(win, span - res, axis=0) if res else win
                for o in range(off + (res - off) % SUBLANES, off + CONV_WIDTH, SUBLANES):
                    q = o - res
                    term = shifted[q:q + _CONV_ROWS, :] * dw_ref[o - off:o - off + 1, cols]
                    acc = term if acc is None else acc + term
            parts.append(acc)
        acc = jnp.concatenate(parts, axis=-1) + bdw
        mean = jnp.mean(acc, axis=-1, keepdims=True)
        d = acc - mean
        var = jnp.mean(d * d, axis=-1, keepdims=True)
        y = d * lax.rsqrt(var + NORM_EPS) * ln_w + ln_b
        o_ref[0, r0:r0 + _CONV_ROWS, :] = (y * _sigmoid(y)).astype(o_ref.dtype)


def _dwconv(z, dw, vec, tm):
    B, S, C = z.shape
    tok = pl.BlockSpec((1, tm, C), lambda b, i: (b, i, 0))
    prev = pl.BlockSpec((1, _CONV_HALO, C), lambda b, i: (b, jnp.maximum(i * (tm // _CONV_HALO) - 1, 0), 0))
    return pl.pallas_call(
        functools.partial(_dwconv_kernel, tm),
        grid=(B, S // tm),
        in_specs=[tok, prev, _full(dw.shape), _full(vec.shape)],
        out_specs=tok,
        out_shape=jax.ShapeDtypeStruct((B, S, C), BF16),
        scratch_shapes=[pltpu.VMEM((_CONV_HALO + tm, C), F32)],
        compiler_params=_cparams("parallel", "parallel"),
        name="conv_dw",
    )(z, z, dw, vec)


def _conv_layer(h, gn, w1, b1, dw, bdw, ln_w, ln_b, tm):
    B, S, D = h.shape
    z = _glu(h.reshape(B * S, D), gn, w1.astype(BF16), b1, _tile(B * S, PROJ_ROWS)).reshape(B, S, -1)
    dw32 = jnp.concatenate([dw, jnp.zeros((32 - CONV_WIDTH, dw.shape[1]), F32)])
    vec = jnp.stack([bdw, ln_w, ln_b] + [jnp.zeros_like(bdw)] * (SUBLANES - 3))
    return _dwconv(z, dw32, vec, tm)


def _rope_table_kernel(pos_ref, frq_ref, cos_ref, sin_ref):
    ang = pos_ref[...] * frq_ref[...]
    cos_ref[...] = jnp.cos(ang)
    sin_ref[...] = jnp.sin(ang)


def _rope_tables(positions):
    B, S = positions.shape
    half = QK_ROPE // 2
    per_row = LANES // half
    rows = B * S // per_row
    inv_freq = ROPE_THETA ** (-jnp.arange(0, QK_ROPE, 2, dtype=F32) / QK_ROPE)
    pos = jnp.repeat(positions.astype(F32).reshape(rows, per_row), half, axis=1)
    frq = jnp.tile(inv_freq, per_row).reshape(1, LANES)
    tr = min(rows, 512)
    cos, sin = pl.pallas_call(
        _rope_table_kernel,
        grid=(rows // tr,),
        in_specs=[pl.BlockSpec((tr, LANES), lambda i: (i, 0)), _full((1, LANES))],
        out_specs=[pl.BlockSpec((tr, LANES), lambda i: (i, 0))] * 2,
        out_shape=[jax.ShapeDtypeStruct((rows, LANES), F32)] * 2,
        compiler_params=_cparams("parallel"),
        name="rope_table",
    )(pos, frq)
    cos = cos.reshape(B * S, half)
    sin = sin.reshape(B * S, half)
    cos_t = jnp.tile(cos, (1, LANES // half))
    sin_t = jnp.tile(jnp.concatenate([-sin, sin], axis=1), (1, LANES // QK_ROPE))
    return cos_t, sin_t


def _mla_proj_kernel(scale, h_ref, gn_ref, wd_ref, gq_ref, gkv_ref, wqn_ref, wqr_ref, wqs_ref, wkn_ref, wv_ref,
                     cos_ref, sin_ref, qn_out, qr_out, kn_out, kr_out, v_out):
    halves = _row_halves(h_ref.shape[0])
    us = [_rms(h_ref[r, :], gn_ref[...]).astype(BF16) for r in halves]
    cs = [_dot(u, wd_ref[...]) for u in us]
    o = Q_LORA + KV_LORA
    reps = qr_out.shape[-1] // LANES
    for r, c in zip(halves, cs):
        cq = _rms(c[:, :Q_LORA], gq_ref[...]).astype(BF16)
        ckv = _rms(c[:, Q_LORA:o], gkv_ref[...]).astype(BF16)
        cos = cos_ref[r, :]
        sin = sin_ref[r, :]
        kr_out[r, :] = (c[:, o:o + LANES] * cos + c[:, o + LANES:o + 2 * LANES] * sin).astype(BF16)
        cos_w = jnp.concatenate([cos] * reps, axis=-1)
        sin_w = jnp.concatenate([sin] * reps, axis=-1)
        qn_out[r, :] = (_dot(cq, wqn_ref[...]) * scale).astype(BF16)
        qr = _dot(cq, wqr_ref[...]) * cos_w + _dot(cq, wqs_ref[...]) * sin_w
        qr_out[r, :] = (qr * scale).astype(BF16)
        kn_out[r, :] = _dot(ckv, wkn_ref[...]).astype(BF16)
        v_out[r, :] = _dot(ckv, wv_ref[...]).astype(BF16)


def _swap_halves(w):
    half = QK_ROPE // 2
    return jnp.concatenate([w[..., half:], w[..., :half]], axis=-1)


def _mla_proj(h2, gn, wd, gq, wuq, gkv, wukv, cos_t, sin_t, tm):
    T, D = h2.shape
    H = MLA_HEADS
    npair = H // 2
    HD = H * QK_NOPE
    kr_w = wd[:, Q_LORA + KV_LORA:]
    zpad = jnp.zeros((D, LANES - 2 * QK_ROPE), F32)
    wd2 = jnp.concatenate([wd[:, :Q_LORA + KV_LORA], kr_w, kr_w, zpad,
                           _swap_halves(kr_w), _swap_halves(kr_w), zpad], axis=1).astype(BF16)
    wq = wuq.reshape(Q_LORA, H, QK_NOPE + QK_ROPE)
    wqn = wq[:, :, :QK_NOPE].reshape(Q_LORA, HD).astype(BF16)
    wq_rope = wq[:, :, QK_NOPE:]

    def pair_pad(w):
        w = w.reshape(Q_LORA, npair, 2 * QK_ROPE)
        w = jnp.concatenate([w, jnp.zeros((Q_LORA, npair, LANES - 2 * QK_ROPE), F32)], axis=-1)
        return w.reshape(Q_LORA, npair * LANES).astype(BF16)

    wqr = pair_pad(wq_rope)
    wqs = pair_pad(_swap_halves(wq_rope))
    wkv = wukv.reshape(KV_LORA, H, QK_NOPE + V_DIM)
    wkn = wkv[:, :, :QK_NOPE].reshape(KV_LORA, HD).astype(BF16)
    wv = wkv[:, :, QK_NOPE:].reshape(KV_LORA, H * V_DIM).astype(BF16)
    scale = math.log2(math.e) / math.sqrt(QK_NOPE + QK_ROPE)
    ws = [wd2, gq.reshape(1, -1), gkv.reshape(1, -1), wqn, wqr, wqs, wkn, wv]
    tokw = lambda n: pl.BlockSpec((tm, n), lambda i: (i, 0))
    return pl.pallas_call(
        functools.partial(_mla_proj_kernel, scale),
        grid=(T // tm,),
        in_specs=[tokw(D), _full((1, D))] + [_full(w.shape) for w in ws] + [tokw(LANES), tokw(LANES)],
        out_specs=[tokw(HD), tokw(npair * LANES), tokw(HD), tokw(LANES), tokw(H * V_DIM)],
        out_shape=[jax.ShapeDtypeStruct((T, HD), BF16), jax.ShapeDtypeStruct((T, npair * LANES), BF16),
                   jax.ShapeDtypeStruct((T, HD), BF16), jax.ShapeDtypeStruct((T, LANES), BF16),
                   jax.ShapeDtypeStruct((T, H * V_DIM), BF16)],
        compiler_params=_cparams("parallel"),
        name="mla_proj",
    )(h2, gn.reshape(1, D), *ws, cos_t, sin_t)


def _attn_kernel(tq, qn_ref, qr_ref, kn_ref, kr_ref, v_ref, o_ref):
    S = qn_ref.shape[1]
    lane = lax.broadcasted_iota(jnp.int32, (1, 2 * LANES), 1)
    head0 = (lane < QK_NOPE) | ((lane >= LANES) & (lane < LANES + QK_ROPE))
    head1 = ((lane >= QK_NOPE) & (lane < LANES)) | ((lane >= LANES + QK_ROPE) & (lane < LANES + 2 * QK_ROPE))
    row = lax.broadcasted_iota(jnp.int32, (2 * tq, tq), 0) % tq
    col = lax.broadcasted_iota(jnp.int32, (2 * tq, tq), 1)
    causal = col <= row
    out_lane = lax.broadcasted_iota(jnp.int32, (1, LANES), 1)

    def scores(i):
        q0, n = i * tq, (i + 1) * tq
        qc = jnp.concatenate([qn_ref[0, q0:n, :], qr_ref[0, q0:n, :]], axis=-1)
        zero = jnp.zeros_like(qc)
        qs = jnp.concatenate([jnp.where(head0, qc, zero), jnp.where(head1, qc, zero)], axis=0)
        kc = jnp.concatenate([kn_ref[0, :n, :], kr_ref[0, :n, :]], axis=-1)
        return lax.dot_general(qs, kc, _NT, preferred_element_type=F32)

    nq = S // tq
    ahead = [scores(i) for i in range(min(ATTN_AHEAD, nq))]
    for i in range(nq):
        q0, n = i * tq, (i + 1) * tq
        s = ahead.pop(0)
        if i + ATTN_AHEAD < nq:
            ahead.append(scores(i + ATTN_AHEAD))
        s_diag = jnp.where(causal, s[:, q0:], -1e30)
        s = jnp.concatenate([s[:, :q0], s_diag], axis=-1) if i else s_diag
        m = jnp.max(s, axis=-1, keepdims=True)
        p = jnp.exp2(s - m)
        l = jnp.sum(p, axis=-1, keepdims=True)
        o = _dot(p.astype(BF16), v_ref[0, :n, :]) / l
        o_ref[0, q0:n, :] = jnp.where(out_lane < V_DIM, o[:tq], o[tq:]).astype(o_ref.dtype)


def _attention(qn, qr, kn, kr, v, tq):
    B, S, HD = qn.shape
    npair = HD // LANES
    spec = pl.BlockSpec((1, S, LANES), lambda b, p: (b, 0, p))
    return pl.pallas_call(
        functools.partial(_attn_kernel, tq),
        grid=(B, npair),
        in_specs=[spec, spec, spec, pl.BlockSpec((1, S, LANES), lambda b, p: (b, 0, 0)), spec],
        out_specs=spec,
        out_shape=jax.ShapeDtypeStruct((B, S, HD), BF16),
        compiler_params=_cparams("parallel", "parallel"),
        name="mla_attn",
    )(qn, qr, kn, kr, v)


def _mla_layer(h, positions, gn, wd, gq, wuq, gkv, wukv, tm, tq):
    B, S, D = h.shape
    cos_t, sin_t = _rope_tables(positions)
    qn, qr, kn, kr, v = _mla_proj(h.reshape(B * S, D), gn, wd, gq, wuq, gkv, wukv, cos_t, sin_t,
                                  _tile(B * S, PROJ_ROWS))
    sh = lambda t: t.reshape(B, S, t.shape[-1])
    return _attention(sh(qn), sh(qr), sh(kn), sh(kr), sh(v), tq)


def _tile(n, pref):
    t = min(n, pref)
    while n % t:
        t //= 2
    return t


def kernel(x, positions, norm_mix, norm_ffn, norm_final, rw_mu, rw_wr, rw_wk, rw_wv, rw_wo, rw_w0, rw_w1, rw_w2, rw_a0, rw_a1, rw_a2, rw_g1, rw_g2, rw_kk, rw_ka, rw_rk, rw_lnx_w, rw_lnx_b, rw_v0, rw_v1, rw_v2, cv_w1, cv_b1, cv_dw, cv_bdw, cv_ln_w, cv_ln_b, cv_w2, cv_b2, ml_wd, ml_gq, ml_wuq, ml_gkv, ml_wukv, ml_wo, ff_w1, ff_w2):
    B, S, D = x.shape
    depth = norm_mix.shape[0]
    tm = _tile(S, 512)
    tq = _tile(S, 256)
    tf = _tile(ff_w1.shape[-1], 1024)
    h = x
    v_first = None
    zero_bias = jnp.zeros((D,), F32)
    ff_w1_bf = ff_w1.astype(BF16)
    ff_w2_bf = ff_w2.astype(BF16)
    ia = ib = ic = 0
    for i in range(depth):
        kind = i % 3
        if kind == 0:
            vres = None if ia == 0 else (rw_v0[ia - 1], rw_v1[ia - 1], rw_v2[ia - 1], v_first)
            y, v = _rwkv_layer(h, norm_mix[i], rw_mu[ia], rw_wr[ia], rw_wk[ia], rw_wv[ia],
                               rw_w0[ia], rw_w1[ia], rw_w2[ia], rw_a0[ia], rw_a1[ia], rw_a2[ia],
                               rw_g1[ia], rw_g2[ia], rw_kk[ia], rw_ka[ia], rw_rk[ia],
                               rw_lnx_w[ia], rw_lnx_b[ia], vres, tm)
            wo, bo = rw_wo[ia], zero_bias
            if ia == 0:
                v_first = v
            ia += 1
        elif kind == 1:
            y = _conv_layer(h, norm_mix[i], cv_w1[ib], cv_b1[ib], cv_dw[ib], cv_bdw[ib],
                            cv_ln_w[ib], cv_ln_b[ib], tm)
            wo, bo = cv_w2[ib], cv_b2[ib]
            ib += 1
        else:
            y = _mla_layer(h, positions, norm_mix[i], ml_wd[ic], ml_gq[ic], ml_wuq[ic],
                           ml_gkv[ic], ml_wukv[ic], tm, tq)
            wo, bo = ml_wo[ic], zero_bias
            ic += 1
        h = _mlp(h.reshape(B * S, D), y.reshape(B * S, -1), wo.astype(BF16), bo, norm_ffn[i],
                 ff_w1_bf, ff_w2_bf, i, norm_final, i == depth - 1, _tile(B * S, MLP_ROWS), tf).reshape(B, S, D)
    return h
```

```python
import functools
import math

import jax
import jax.numpy as jnp
from jax import lax
from jax.experimental import pallas as pl
from jax.experimental.pallas import tpu as pltpu

F32 = jnp.float32
BF16 = jnp.bfloat16

NORM_EPS = 1e-5
GN_EPS = 64e-5
RWKV_HEAD = 64
CONV_WIDTH = 31
MLA_HEADS = 16
QK_NOPE = 64
QK_ROPE = 32
V_DIM = 64
Q_LORA = 384
KV_LORA = 256
ROPE_THETA = 10000.0

LANES = 128
SUBLANES = 8
MXU_COLS = 256
WKV_CHUNK = 64
WKV_ROWS = 8
RWKV_PROJ_SPLIT = 1
MLP_ROWS = 1024
PROJ_ROWS = 1024
ATTN_AHEAD = 1
VMEM_LIMIT_BYTES = 56 * 1024 * 1024

_NT = (((1,), (1,)), ((), ()))
_TN = (((0,), (0,)), ((), ()))


def _cparams(*sem):
    return pltpu.CompilerParams(dimension_semantics=sem, vmem_limit_bytes=VMEM_LIMIT_BYTES)


def _dot(a, b):
    return jnp.dot(a, b, preferred_element_type=F32)


def _rms(x, g):
    ms = jnp.mean(x * x, axis=-1, keepdims=True)
    return x * lax.rsqrt(ms + NORM_EPS) * g


def _sigmoid(x):
    return 1.0 / (1.0 + jnp.exp(-x))


def _softplus(x):
    return jnp.maximum(x, 0.0) + jnp.log(1.0 + jnp.exp(-jnp.abs(x)))


def _head_sum(x, bd):
    outs = []
    for c in range(x.shape[-1] // MXU_COLS):
        outs.append(_dot(x[:, c * MXU_COLS:(c + 1) * MXU_COLS].astype(BF16), bd))
    return jnp.concatenate(outs, axis=-1)


def _row_halves(n):
    return [slice(0, n // 2), slice(n // 2, n)]


def _full(shape):
    n = len(shape)
    return pl.BlockSpec(shape, lambda *_: (0,) * n)


def _mlp_kernel(final_norm, tf, h_ref, y_ref, wo_ref, bo_ref, g_ref, w1_ref, w2_ref, gf_ref, o_ref, a_scr):
    h = h_ref[...] + bo_ref[...] + _dot(y_ref[...], wo_ref[...])
    u = _rms(h, g_ref[...]).astype(BF16)
    for c in range(a_scr.shape[1] // tf):
        cols = slice(c * tf, (c + 1) * tf)
        a = _dot(u, w1_ref[:, cols])
        a_scr[:, cols] = jnp.square(jnp.maximum(a, 0.0)).astype(BF16)
    out = h + _dot(a_scr[...], w2_ref[...])
    if final_norm:
        out = _rms(out, gf_ref[...])
    o_ref[...] = out


def _resident(shape):
    n = len(shape)
    return pl.BlockSpec(shape, lambda *_: (0,) * n, pipeline_mode=pl.Buffered(1))


def _mlp(h2, y2, wo, bo, g, w1_all, w2_all, layer, gf, final_norm, tm, tf):
    T, D = h2.shape
    K = y2.shape[1]
    F = w1_all.shape[2]
    one = pl.Buffered(1)
    return pl.pallas_call(
        functools.partial(_mlp_kernel, final_norm, tf),
        grid=(T // tm,),
        in_specs=[
            pl.BlockSpec((tm, D), lambda i: (i, 0)),
            pl.BlockSpec((tm, K), lambda i: (i, 0)),
            _resident((K, D)),
            _full((1, D)),
            _full((1, D)),
            pl.BlockSpec((None, D, F), lambda i: (layer, 0, 0), pipeline_mode=one),
            pl.BlockSpec((None, F, D), lambda i: (layer, 0, 0), pipeline_mode=one),
            _full((1, D)),
        ],
        out_specs=pl.BlockSpec((tm, D), lambda i: (i, 0)),
        out_shape=jax.ShapeDtypeStruct((T, D), F32),
        scratch_shapes=[pltpu.VMEM((tm, F), BF16)],
        compiler_params=_cparams("parallel"),
        name="mlp",
    )(h2, y2, wo, bo.reshape(1, D), g.reshape(1, D), w1_all, w2_all, gf.reshape(1, D))


def _rwkv_proj_kernel(has_vres, tm, *refs):
    if has_vres:
        (h_ref, hp_ref, gn_ref, mu_ref, vec_ref, wr_ref, wk_ref, wv_ref, w1_ref, w2_ref, a1_ref, a2_ref,
         g1_ref, g2_ref, bd_ref, v1_ref, v2_ref, vf_ref,
         r_out, lw_out, k_out, v_out, a_out, b_out, g_out) = refs
    else:
        (h_ref, hp_ref, gn_ref, mu_ref, vec_ref, wr_ref, wk_ref, wv_ref, w1_ref, w2_ref, a1_ref, a2_ref,
         g1_ref, g2_ref, bd_ref,
         r_out, lw_out, k_out, v_out, a_out, b_out, g_out) = refs
    i = pl.program_id(1)
    gn = gn_ref[...]
    mu = mu_ref[...]
    vec = vec_ref[...]
    w0, a0, k_k, k_a = vec[0:1], vec[1:2], vec[2:3], vec[3:4]
    prev = _rms(hp_ref[0], gn)[SUBLANES - 1:SUBLANES, :]
    prev = jnp.where(i > 0, prev, 0.0)
    nrows = tm // RWKV_PROJ_SPLIT
    row = lax.broadcasted_iota(jnp.int32, (nrows, 1), 0)
    for s in range(RWKV_PROJ_SPLIT):
        rows = slice(s * nrows, (s + 1) * nrows)
        u = _rms(h_ref[0, rows, :], gn)
        shifted = jnp.where(row == 0, prev, pltpu.roll(u, 1, axis=0))
        prev = u[nrows - 1:nrows, :]
        xx = shifted - u

        def mix(j):
            return (u + xx * mu[j:j + 1, :]).astype(BF16)

        xr, xw, xk, xv, xa, xg = [mix(j) for j in range(6)]
        lo_w = _dot(xw, w1_ref[...])
        lo_a = _dot(xa, a1_ref[...])
        lo_g = _dot(xg, g1_ref[...])
        if has_vres:
            lo_v = _dot(xv, v1_ref[...])
        wl = w0 + _dot(jnp.tanh(lo_w).astype(BF16), w2_ref[...])
        a_pre = a0 + _dot(lo_a.astype(BF16), a2_ref[...])
        g = _dot(_sigmoid(lo_g).astype(BF16), g2_ref[...])
        if has_vres:
            gate_pre = vec[4:5] + _dot(lo_v.astype(BF16), v2_ref[...])
        k = _dot(xk, wk_ref[...])
        v = _dot(xv, wv_ref[...])
        r = _dot(xr, wr_ref[...])
        w = -_softplus(-wl) - 0.5
        a = _sigmoid(a_pre)
        if has_vres:
            v = v + (vf_ref[0, rows, :] - v) * _sigmoid(gate_pre)
        kk = k * k_k
        ss = _head_sum(kk * kk, bd_ref[...])
        kk = kk * lax.rsqrt(jnp.maximum(ss, 1e-24))
        r_out[0, rows, :] = r
        lw_out[0, rows, :] = -jnp.exp(w)
        k_out[0, rows, :] = k * (1.0 + (a - 1.0) * k_a)
        v_out[0, rows, :] = v
        a_out[0, rows, :] = -kk
        b_out[0, rows, :] = kk * a
        g_out[0, rows, :] = g


def _block_diag_ones(n, blk):
    idx = jnp.arange(n) // blk
    return (idx[:, None] == idx[None, :]).astype(BF16)


def _rwkv_proj(h, gn, mu, vec, ws, vres, tm):
    B, S, D = h.shape
    has_vres = vres is not None
    tok = pl.BlockSpec((1, tm, D), lambda b, i: (b, i, 0))
    prev = pl.BlockSpec((1, SUBLANES, D), lambda b, i: (b, jnp.maximum(i * (tm // SUBLANES) - 1, 0), 0))
    args = [h, h, gn.reshape(1, D), mu, vec] + list(ws) + [_block_diag_ones(MXU_COLS, RWKV_HEAD)]
    specs = [tok, prev, _full((1, D)), _full(mu.shape), _full(vec.shape)] + [_full(w.shape) for w in ws]
    specs.append(_full((MXU_COLS, MXU_COLS)))
    if has_vres:
        v1, v2, v_first = vres
        args += [v1, v2, v_first]
        specs += [_full(v1.shape), _full(v2.shape), tok]
    out = jax.ShapeDtypeStruct((B, S, D), F32)
    return pl.pallas_call(
        functools.partial(_rwkv_proj_kernel, has_vres, tm),
        grid=(B, S // tm),
        in_specs=specs,
        out_specs=[tok] * 7,
        out_shape=[out] * 7,
        compiler_params=_cparams("parallel", "parallel"),
        name="rwkv_proj",
    )(*args)


def _wkv_kernel(r_ref, lw_ref, k_ref, v_ref, a_ref, b_ref, g_ref, vec_ref, tri_ref, bd_ref,
                y_ref, h_scr, y_scr):
    L = WKV_CHUNK
    nb, _, D = r_ref.shape
    npair = D // LANES

    @pl.when(pl.program_id(1) == 0)
    def _():
        h_scr[...] = jnp.zeros_like(h_scr)

    lane = lax.broadcasted_iota(jnp.int32, (1, LANES), 1)
    first = lane < RWKV_HEAD
    t_idx = lax.broadcasted_iota(jnp.int32, (L, LANES), 0)
    s_idx = lax.broadcasted_iota(jnp.int32, (L, LANES), 1) % RWKV_HEAD
    lower = t_idx > s_idx
    lower_incl = t_idx >= s_idx
    eye = (t_idx == s_idx).astype(F32)
    same_head = (lax.broadcasted_iota(jnp.int32, (LANES, LANES), 0) // RWKV_HEAD
                 == lax.broadcasted_iota(jnp.int32, (LANES, LANES), 1) // RWKV_HEAD)

    def stack(x):
        return jnp.concatenate([jnp.where(first, x, 0.0), jnp.where(first, 0.0, x)], axis=0).astype(BF16)

    ar, bk, blkl, v_b, v_s, h0, w_end = [], [], [], [], [], [], []
    for bi in range(nb):
        lw = lw_ref[bi]
        hi = lw.astype(BF16)
        low = (lw - hi.astype(F32)).astype(BF16)
        cum = _dot(tri_ref[...], jnp.concatenate([hi, low], axis=0))
        e_pos = jnp.exp(cum)
        e_neg = jnp.exp(-cum)
        rt = r_ref[bi] * e_pos
        at = a_ref[bi] * jnp.exp(cum - lw)
        kt = k_ref[bi] * e_neg
        bt = b_ref[bi] * e_neg
        we = e_pos[L - 1:L, :]
        kl = kt * we
        bl = bt * we
        v = v_ref[bi]
        for p in range(npair):
            sl = slice(p * LANES, (p + 1) * LANES)
            ar.append(jnp.concatenate([at[:, sl], rt[:, sl]], axis=0).astype(BF16))
            bk.append(jnp.concatenate([stack(bt[:, sl]), stack(kt[:, sl])], axis=0))
            blkl.append(jnp.concatenate([bl[:, sl], kl[:, sl]], axis=0).astype(BF16))
            v_b.append(v[:, sl].astype(BF16))
            v_s.append(stack(v[:, sl]))
            h0.append(h_scr[bi * npair + p])
            w_end.append(we[:, sl])

    sc = [lax.dot_general(x, y, _NT, preferred_element_type=F32) for x, y in zip(ar, bk)]
    m_ab = [jnp.where(lower, s[:L, :LANES], 0.0) for s in sc]
    m_rb = [jnp.where(lower_incl, s[L:, :LANES], 0.0).astype(BF16) for s in sc]
    m_k = [jnp.concatenate([jnp.where(lower, s[:L, LANES:], 0.0), jnp.where(lower_incl, s[L:, LANES:], 0.0)],
                           axis=0).astype(BF16) for s in sc]
    xy = [_dot(jnp.concatenate([x, m], axis=1), jnp.concatenate([h.astype(BF16), vs], axis=0))
          for x, m, h, vs in zip(ar, m_k, h0, v_s)]
    t_inv = [eye + m for m in m_ab]
    a_cur = [_dot(m.astype(BF16), stack(m)) for m in m_ab]
    n = 4
    while n < L:
        res = [_dot(m.astype(BF16), jnp.concatenate([stack(m), stack(t)], axis=1)) for m, t in zip(a_cur, t_inv)]
        t_inv = [t + x[:, LANES:] for t, x in zip(t_inv, res)]
        a_cur = [x[:, :LANES] for x in res]
        n *= 2
    t_inv = [t + _dot(m.astype(BF16), stack(t)) for t, m in zip(t_inv, a_cur)]
    u = [_dot(t.astype(BF16), stack(x[:L])) for t, x in zip(t_inv, xy)]
    ys = [x[L:] + _dot(m, stack(uu)) for x, m, uu in zip(xy, m_rb, u)]
    dh = [lax.dot_general(w, jnp.concatenate([uu.astype(BF16), vb], axis=0), _TN, preferred_element_type=F32)
          for w, uu, vb in zip(blkl, u, v_b)]
    for j in range(nb * npair):
        bi, p = divmod(j, npair)
        y_scr[bi, :, p * LANES:(p + 1) * LANES] = ys[j]
        decay = jnp.transpose(jnp.broadcast_to(w_end[j], (LANES, LANES)))
        h_scr[j] = h0[j] * decay + jnp.where(same_head, dh[j], 0.0)

    vec = vec_ref[...]
    r_k, ln_w, ln_b = vec[0:1], vec[1:2], vec[2:3]
    bd = bd_ref[...]
    inv_n = 1.0 / RWKV_HEAD
    for bi in range(nb):
        y = y_scr[bi]
        d = y - _head_sum(y, bd) * inv_n
        var = _head_sum(d * d, bd) * inv_n
        yn = d * lax.rsqrt(var + GN_EPS) * ln_w + ln_b
        bonus = _head_sum(r_ref[bi] * k_ref[bi] * r_k, bd) * v_ref[bi]
        y_ref[bi] = ((yn + bonus) * g_ref[bi]).astype(y_ref.dtype)


def _wkv(r, lw, k, v, a, b, g, vec, nb):
    B, S, D = r.shape
    L = WKV_CHUNK
    tok = pl.BlockSpec((nb, L, D), lambda bi, c: (bi, c, 0))
    tri = (jnp.arange(L)[:, None] >= jnp.arange(L)[None, :]).astype(BF16)
    tri = jnp.concatenate([tri, tri], axis=1)
    return pl.pallas_call(
        _wkv_kernel,
        grid=(B // nb, S // L),
        in_specs=[tok] * 7 + [_full(vec.shape), _full((L, 2 * L)), _full((MXU_COLS, MXU_COLS))],
        out_specs=tok,
        out_shape=jax.ShapeDtypeStruct((B, S, D), BF16),
        scratch_shapes=[pltpu.VMEM((nb * D // LANES, LANES, LANES), F32), pltpu.VMEM((nb, L, D), F32)],
        compiler_params=_cparams("parallel", "arbitrary"),
        name="wkv",
    )(r, lw, k, v, a, b, g, vec, tri, _block_diag_ones(MXU_COLS, RWKV_HEAD))


def _rwkv_layer(h, gn, mu, wr, wk, wv, w0, w1, w2, a0, a1, a2, g1, g2, k_k, k_a, r_k, lnx_w, lnx_b,
                vres, tm):
    B, S, D = h.shape
    rows = [w0, a0, k_k, k_a]
    extra = None
    if vres is not None:
        v0, v1, v2, v_first = vres
        rows.append(v0)
        extra = (v1.astype(BF16), v2.astype(BF16), v_first)
    vec = jnp.stack(rows + [jnp.zeros_like(w0)] * (SUBLANES - len(rows)))
    mu8 = jnp.concatenate([mu, jnp.zeros((SUBLANES - mu.shape[0], D), F32)])
    ws = [w.astype(BF16) for w in (wr, wk, wv, w1, w2, a1, a2, g1, g2)]
    r, lw, k, v, a, b, g = _rwkv_proj(h, gn, mu8, vec, ws, extra, tm)
    vec2 = jnp.stack([r_k.reshape(D), lnx_w, lnx_b] + [jnp.zeros_like(w0)] * (SUBLANES - 3))
    return _wkv(r, lw, k, v, a, b, g, vec2, _tile(B, WKV_ROWS)), v


def _glu_kernel(h_ref, g_ref, w_ref, b_ref, z_ref):
    C = z_ref.shape[-1]
    halves = _row_halves(h_ref.shape[0])
    us = [_rms(h_ref[r, :], g_ref[...]).astype(BF16) for r in halves]
    zs = [_dot(u, w_ref[...]) for u in us]
    for r, z in zip(halves, zs):
        z = z + b_ref[...]
        z_ref[r, :] = z[:, :C] * _sigmoid(z[:, C:])


def _glu(h2, g, w1, b1, tm):
    T, D = h2.shape
    N = w1.shape[1]
    return pl.pallas_call(
        _glu_kernel,
        grid=(T // tm,),
        in_specs=[pl.BlockSpec((tm, D), lambda i: (i, 0)), _full((1, D)), _full((D, N)), _full((1, N))],
        out_specs=pl.BlockSpec((tm, N // 2), lambda i: (i, 0)),
        out_shape=jax.ShapeDtypeStruct((T, N // 2), F32),
        compiler_params=_cparams("parallel"),
        name="conv_glu",
    )(h2, g.reshape(1, D), w1, b1.reshape(1, N))


_CONV_HALO = 32


_CONV_ROWS = 64
_CONV_COLS = 128


def _dwconv_kernel(tm, z_ref, zp_ref, dw_ref, vec_ref, o_ref, zbuf):
    i = pl.program_id(1)
    C = z_ref.shape[-1]
    zbuf[0:_CONV_HALO, :] = jnp.where(i > 0, zp_ref[0], 0.0)
    zbuf[_CONV_HALO:, :] = z_ref[0]
    vec = vec_ref[...]
    bdw, ln_w, ln_b = vec[0:1], vec[1:2], vec[2:3]
    off = _CONV_HALO - (CONV_WIDTH - 1)
    span = _CONV_ROWS + _CONV_HALO
    for r0 in range(0, tm, _CONV_ROWS):
        parts = []
        for c0 in range(0, C, _CONV_COLS):
            cols = slice(c0, c0 + _CONV_COLS)
            acc = None
            win = zbuf[r0:r0 + span, cols]
            for res in range(SUBLANES):
                shifted = pltpu.roll(win, span - res, axis=0) if res else win
                for o in range(off + (res - off) % SUBLANES, off + CONV_WIDTH, SUBLANES):
                    q = o - res
                    term = shifted[q:q + _CONV_ROWS, :] * dw_ref[o - off:o - off + 1, cols]
                    acc = term if acc is None else acc + term
            parts.append(acc)
        acc = jnp.concatenate(parts, axis=-1) + bdw
        mean = jnp.mean(acc, axis=-1, keepdims=True)
        d = acc - mean
        var = jnp.mean(d * d, axis=-1, keepdims=True)
        y = d * lax.rsqrt(var + NORM_EPS) * ln_w + ln_b
        o_ref[0, r0:r0 + _CONV_ROWS, :] = (y * _sigmoid(y)).astype(o_ref.dtype)


def _dwconv(z, dw, vec, tm):
    B, S, C = z.shape
    tok = pl.BlockSpec((1, tm, C), lambda b, i: (b, i, 0))
    prev = pl.BlockSpec((1, _CONV_HALO, C), lambda b, i: (b, jnp.maximum(i * (tm // _CONV_HALO) - 1, 0), 0))
    return pl.pallas_call(
        functools.partial(_dwconv_kernel, tm),
        grid=(B, S // tm),
        in_specs=[tok, prev, _full(dw.shape), _full(vec.shape)],
        out_specs=tok,
        out_shape=jax.ShapeDtypeStruct((B, S, C), BF16),
        scratch_shapes=[pltpu.VMEM((_CONV_HALO + tm, C), F32)],
        compiler_params=_cparams("parallel", "parallel"),
        name="conv_dw",
    )(z, z, dw, vec)


def _conv_layer(h, gn, w1, b1, dw, bdw, ln_w, ln_b, tm):
    B, S, D = h.shape
    z = _glu(h.reshape(B * S, D), gn, w1.astype(BF16), b1, _tile(B * S, PROJ_ROWS)).reshape(B, S, -1)
    dw32 = jnp.concatenate([dw, jnp.zeros((32 - CONV_WIDTH, dw.shape[1]), F32)])
    vec = jnp.stack([bdw, ln_w, ln_b] + [jnp.zeros_like(bdw)] * (SUBLANES - 3))
    return _dwconv(z, dw32, vec, tm)


def _rope_table_kernel(pos_ref, frq_ref, cos_ref, sin_ref):
    ang = pos_ref[...] * frq_ref[...]
    cos_ref[...] = jnp.cos(ang)
    sin_ref[...] = jnp.sin(ang)


def _rope_tables(positions):
    B, S = positions.shape
    half = QK_ROPE // 2
    per_row = LANES // half
    rows = B * S // per_row
    inv_freq = ROPE_THETA ** (-jnp.arange(0, QK_ROPE, 2, dtype=F32) / QK_ROPE)
    pos = jnp.repeat(positions.astype(F32).reshape(rows, per_row), half, axis=1)
    frq = jnp.tile(inv_freq, per_row).reshape(1, LANES)
    tr = min(rows, 512)
    cos, sin = pl.pallas_call(
        _rope_table_kernel,
        grid=(rows // tr,),
        in_specs=[pl.BlockSpec((tr, LANES), lambda i: (i, 0)), _full((1, LANES))],
        out_specs=[pl.BlockSpec((tr, LANES), lambda i: (i, 0))] * 2,
        out_shape=[jax.ShapeDtypeStruct((rows, LANES), F32)] * 2,
        compiler_params=_cparams("parallel"),
        name="rope_table",
    )(pos, frq)
    cos = cos.reshape(B * S, half)
    sin = sin.reshape(B * S, half)
    cos_t = jnp.tile(cos, (1, LANES // half))
    sin_t = jnp.tile(jnp.concatenate([-sin, sin], axis=1), (1, LANES // QK_ROPE))
    return cos_t, sin_t


def _mla_proj_kernel(scale, h_ref, gn_ref, wd_ref, gq_ref, gkv_ref, wqn_ref, wqr_ref, wqs_ref, wkn_ref, wv_ref,
                     cos_ref, sin_ref, qn_out, qr_out, kn_out, kr_out, v_out):
    halves = _row_halves(h_ref.shape[0])
    us = [_rms(h_ref[r, :], gn_ref[...]).astype(BF16) for r in halves]
    cs = [_dot(u, wd_ref[...]) for u in us]
    o = Q_LORA + KV_LORA
    reps = qr_out.shape[-1] // LANES
    for r, c in zip(halves, cs):
        cq = _rms(c[:, :Q_LORA], gq_ref[...]).astype(BF16)
        ckv = _rms(c[:, Q_LORA:o], gkv_ref[...]).astype(BF16)
        cos = cos_ref[r, :]
        sin = sin_ref[r, :]
        kr_out[r, :] = (c[:, o:o + LANES] * cos + c[:, o + LANES:o + 2 * LANES] * sin).astype(BF16)
        cos_w = jnp.concatenate([cos] * reps, axis=-1)
        sin_w = jnp.concatenate([sin] * reps, axis=-1)
        qn_out[r, :] = (_dot(cq, wqn_ref[...]) * scale).astype(BF16)
        qr = _dot(cq, wqr_ref[...]) * cos_w + _dot(cq, wqs_ref[...]) * sin_w
        qr_out[r, :] = (qr * scale).astype(BF16)
        kn_out[r, :] = _dot(ckv, wkn_ref[...]).astype(BF16)
        v_out[r, :] = _dot(ckv, wv_ref[...]).astype(BF16)


def _swap_halves(w):
    half = QK_ROPE // 2
    return jnp.concatenate([w[..., half:], w[..., :half]], axis=-1)


def _mla_proj(h2, gn, wd, gq, wuq, gkv, wukv, cos_t, sin_t, tm):
    T, D = h2.shape
    H = MLA_HEADS
    npair = H // 2
    HD = H * QK_NOPE
    kr_w = wd[:, Q_LORA + KV_LORA:]
    zpad = jnp.zeros((D, LANES - 2 * QK_ROPE), F32)
    wd2 = jnp.concatenate([wd[:, :Q_LORA + KV_LORA], kr_w, kr_w, zpad,
                           _swap_halves(kr_w), _swap_halves(kr_w), zpad], axis=1).astype(BF16)
    wq = wuq.reshape(Q_LORA, H, QK_NOPE + QK_ROPE)
    wqn = wq[:, :, :QK_NOPE].reshape(Q_LORA, HD).astype(BF16)
    wq_rope = wq[:, :, QK_NOPE:]

    def pair_pad(w):
        w = w.reshape(Q_LORA, npair, 2 * QK_ROPE)
        w = jnp.concatenate([w, jnp.zeros((Q_LORA, npair, LANES - 2 * QK_ROPE), F32)], axis=-1)
        return w.reshape(Q_LORA, npair * LANES).astype(BF16)

    wqr = pair_pad(wq_rope)
    wqs = pair_pad(_swap_halves(wq_rope))
    wkv = wukv.reshape(KV_LORA, H, QK_NOPE + V_DIM)
    wkn = wkv[:, :, :QK_NOPE].reshape(KV_LORA, HD).astype(BF16)
    wv = wkv[:, :, QK_NOPE:].reshape(KV_LORA, H * V_DIM).astype(BF16)
    scale = math.log2(math.e) / math.sqrt(QK_NOPE + QK_ROPE)
    ws = [wd2, gq.reshape(1, -1), gkv.reshape(1, -1), wqn, wqr, wqs, wkn, wv]
    tokw = lambda n: pl.BlockSpec((tm, n), lambda i: (i, 0))
    return pl.pallas_call(
        functools.partial(_mla_proj_kernel, scale),
        grid=(T // tm,),
        in_specs=[tokw(D), _full((1, D))] + [_full(w.shape) for w in ws] + [tokw(LANES), tokw(LANES)],
        out_specs=[tokw(HD), tokw(npair * LANES), tokw(HD), tokw(LANES), tokw(H * V_DIM)],
        out_shape=[jax.ShapeDtypeStruct((T, HD), BF16), jax.ShapeDtypeStruct((T, npair * LANES), BF16),
                   jax.ShapeDtypeStruct((T, HD), BF16), jax.ShapeDtypeStruct((T, LANES), BF16),
                   jax.ShapeDtypeStruct((T, H * V_DIM), BF16)],
        compiler_params=_cparams("parallel"),
        name="mla_proj",
    )(h2, gn.reshape(1, D), *ws, cos_t, sin_t)


def _attn_kernel(tq, qn_ref, qr_ref, kn_ref, kr_ref, v_ref, o_ref):
    S = qn_ref.shape[1]
    lane = lax.broadcasted_iota(jnp.int32, (1, 2 * LANES), 1)
    head0 = (lane < QK_NOPE) | ((lane >= LANES) & (lane < LANES + QK_ROPE))
    head1 = ((lane >= QK_NOPE) & (lane < LANES)) | ((lane >= LANES + QK_ROPE) & (lane < LANES + 2 * QK_ROPE))
    row = lax.broadcasted_iota(jnp.int32, (2 * tq, tq), 0) % tq
    col = lax.broadcasted_iota(jnp.int32, (2 * tq, tq), 1)
    causal = col <= row
    out_lane = lax.broadcasted_iota(jnp.int32, (1, LANES), 1)

    def scores(i):
        q0, n = i * tq, (i + 1) * tq
        qc = jnp.concatenate([qn_ref[0, q0:n, :], qr_ref[0, q0:n, :]], axis=-1)
        zero = jnp.zeros_like(qc)
        qs = jnp.concatenate([jnp.where(head0, qc, zero), jnp.where(head1, qc, zero)], axis=0)
        kc = jnp.concatenate([kn_ref[0, :n, :], kr_ref[0, :n, :]], axis=-1)
        return lax.dot_general(qs, kc, _NT, preferred_element_type=F32)

    nq = S // tq
    ahead = [scores(i) for i in range(min(ATTN_AHEAD, nq))]
    for i in range(nq):
        q0, n = i * tq, (i + 1) * tq
        s = ahead.pop(0)
        if i + ATTN_AHEAD < nq:
            ahead.append(scores(i + ATTN_AHEAD))
        s_diag = jnp.where(causal, s[:, q0:], -1e30)
        s = jnp.concatenate([s[:, :q0], s_diag], axis=-1) if i else s_diag
        m = jnp.max(s, axis=-1, keepdims=True)
        p = jnp.exp2(s - m)
        l = jnp.sum(p, axis=-1, keepdims=True)
        o = _dot(p.astype(BF16), v_ref[0, :n, :]) / l
        o_ref[0, q0:n, :] = jnp.where(out_lane < V_DIM, o[:tq], o[tq:]).astype(o_ref.dtype)


def _attention(qn, qr, kn, kr, v, tq):
    B, S, HD = qn.shape
    npair = HD // LANES
    spec = pl.BlockSpec((1, S, LANES), lambda b, p: (b, 0, p))
    return pl.pallas_call(
        functools.partial(_attn_kernel, tq),
        grid=(B, npair),
        in_specs=[spec, spec, spec, pl.BlockSpec((1, S, LANES), lambda b, p: (b, 0, 0)), spec],
        out_specs=spec,
        out_shape=jax.ShapeDtypeStruct((B, S, HD), BF16),
        compiler_params=_cparams("parallel", "parallel"),
        name="mla_attn",
    )(qn, qr, kn, kr, v)


def _mla_layer(h, positions, gn, wd, gq, wuq, gkv, wukv, tm, tq):
    B, S, D = h.shape
    cos_t, sin_t = _rope_tables(positions)
    qn, qr, kn, kr, v = _mla_proj(h.reshape(B * S, D), gn, wd, gq, wuq, gkv, wukv, cos_t, sin_t,
                                  _tile(B * S, PROJ_ROWS))
    sh = lambda t: t.reshape(B, S, t.shape[-1])
    return _attention(sh(qn), sh(qr), sh(kn), sh(kr), sh(v), tq)


def _tile(n, pref):
    t = min(n, pref)
    while n % t:
        t //= 2
    return t


def kernel(x, positions, norm_mix, norm_ffn, norm_final, rw_mu, rw_wr, rw_wk, rw_wv, rw_wo, rw_w0, rw_w1, rw_w2, rw_a0, rw_a1, rw_a2, rw_g1, rw_g2, rw_kk, rw_ka, rw_rk, rw_lnx_w, rw_lnx_b, rw_v0, rw_v1, rw_v2, cv_w1, cv_b1, cv_dw, cv_bdw, cv_ln_w, cv_ln_b, cv_w2, cv_b2, ml_wd, ml_gq, ml_wuq, ml_gkv, ml_wukv, ml_wo, ff_w1, ff_w2):
    B, S, D = x.shape
    depth = norm_mix.shape[0]
    tm = _tile(S, 512)
    tq = _tile(S, 256)
    tf = _tile(ff_w1.shape[-1], 1024)
    h = x
    v_first = None
    zero_bias = jnp.zeros((D,), F32)
    ff_w1_bf = ff_w1.astype(BF16)
    ff_w2_bf = ff_w2.astype(BF16)
    ia = ib = ic = 0
    for i in range(depth):
        kind = i % 3
        if kind == 0:
            vres = None if ia == 0 else (rw_v0[ia - 1], rw_v1[ia - 1], rw_v2[ia - 1], v_first)
            y, v = _rwkv_layer(h, norm_mix[i], rw_mu[ia], rw_wr[ia], rw_wk[ia], rw_wv[ia],
                               rw_w0[ia], rw_w1[ia], rw_w2[ia], rw_a0[ia], rw_a1[ia], rw_a2[ia],
                               rw_g1[ia], rw_g2[ia], rw_kk[ia], rw_ka[ia], rw_rk[ia],
                               rw_lnx_w[ia], rw_lnx_b[ia], vres, tm)
            wo, bo = rw_wo[ia], zero_bias
            if ia == 0:
                v_first = v
            ia += 1
        elif kind == 1:
            y = _conv_layer(h, norm_mix[i], cv_w1[ib], cv_b1[ib], cv_dw[ib], cv_bdw[ib],
                            cv_ln_w[ib], cv_ln_b[ib], tm)
            wo, bo = cv_w2[ib], cv_b2[ib]
            ib += 1
        else:
            y = _mla_layer(h, positions, norm_mix[i], ml_wd[ic], ml_gq[ic], ml_wuq[ic],
                           ml_gkv[ic], ml_wukv[ic], tm, tq)
            wo, bo = ml_wo[ic], zero_bias
            ic += 1
        h = _mlp(h.reshape(B * S, D), y.reshape(B * S, -1), wo.astype(BF16), bo, norm_ffn[i],
                 ff_w1_bf, ff_w2_bf, i, norm_final, i == depth - 1, _tile(B * S, MLP_ROWS), tf).reshape(B, S, D)
    return h
```

```python
import functools
import math

import jax
import jax.numpy as jnp
from jax import lax
from jax.experimental import pallas as pl
from jax.experimental.pallas import tpu as pltpu

F32 = jnp.float32
BF16 = jnp.bfloat16

NORM_EPS = 1e-5
GN_EPS = 64e-5
RWKV_HEAD = 64
CONV_WIDTH = 31
MLA_HEADS = 16
QK_NOPE = 64
QK_ROPE = 32
V_DIM = 64
Q_LORA = 384
KV_LORA = 256
ROPE_THETA = 10000.0

LANES = 128
SUBLANES = 8
MXU_COLS = 256
WKV_CHUNK = 64
WKV_ROWS = 8
RWKV_PROJ_SPLIT = 1
MLP_ROWS = 1024
PROJ_ROWS = 1024
ATTN_AHEAD = 1
VMEM_LIMIT_BYTES = 56 * 1024 * 1024

_NT = (((1,), (1,)), ((), ()))
_TN = (((0,), (0,)), ((), ()))


def _cparams(*sem):
    return pltpu.CompilerParams(dimension_semantics=sem, vmem_limit_bytes=VMEM_LIMIT_BYTES)


def _dot(a, b):
    return jnp.dot(a, b, preferred_element_type=F32)


def _rms(x, g):
    ms = jnp.mean(x * x, axis=-1, keepdims=True)
    return x * lax.rsqrt(ms + NORM_EPS) * g


def _sigmoid(x):
    return 1.0 / (1.0 + jnp.exp(-x))


def _softplus(x):
    return jnp.maximum(x, 0.0) + jnp.log(1.0 + jnp.exp(-jnp.abs(x)))


def _head_sum(x, bd):
    outs = []
    for c in range(x.shape[-1] // MXU_COLS):
        outs.append(_dot(x[:, c * MXU_COLS:(c + 1) * MXU_COLS].astype(BF16), bd))
    return jnp.concatenate(outs, axis=-1)


def _row_halves(n):
    return [slice(0, n // 2), slice(n // 2, n)]


def _full(shape):
    n = len(shape)
    return pl.BlockSpec(shape, lambda *_: (0,) * n)


def _mlp_kernel(final_norm, tf, h_ref, y_ref, wo_ref, bo_ref, g_ref, w1_ref, w2_ref, gf_ref, o_ref, a_scr):
    h = h_ref[...] + bo_ref[...] + _dot(y_ref[...], wo_ref[...])
    u = _rms(h, g_ref[...]).astype(BF16)
    for c in range(a_scr.shape[1] // tf):
        cols = slice(c * tf, (c + 1) * tf)
        a = _dot(u, w1_ref[:, cols])
        a_scr[:, cols] = jnp.square(jnp.maximum(a, 0.0)).astype(BF16)
    out = h + _dot(a_scr[...], w2_ref[...])
    if final_norm:
        out = _rms(out, gf_ref[...])
    o_ref[...] = out


def _resident(shape):
    n = len(shape)
    return pl.BlockSpec(shape, lambda *_: (0,) * n, pipeline_mode=pl.Buffered(1))


def _mlp(h2, y2, wo, bo, g, w1_all, w2_all, layer, gf, final_norm, tm, tf):
    T, D = h2.shape
    K = y2.shape[1]
    F = w1_all.shape[2]
    one = pl.Buffered(1)
    return pl.pallas_call(
        functools.partial(_mlp_kernel, final_norm, tf),
        grid=(T // tm,),
        in_specs=[
            pl.BlockSpec((tm, D), lambda i: (i, 0)),
            pl.BlockSpec((tm, K), lambda i: (i, 0)),
            _resident((K, D)),
            _full((1, D)),
            _full((1, D)),
            pl.BlockSpec((None, D, F), lambda i: (layer, 0, 0), pipeline_mode=one),
            pl.BlockSpec((None, F, D), lambda i: (layer, 0, 0), pipeline_mode=one),
            _full((1, D)),
        ],
        out_specs=pl.BlockSpec((tm, D), lambda i: (i, 0)),
        out_shape=jax.ShapeDtypeStruct((T, D), F32),
        scratch_shapes=[pltpu.VMEM((tm, F), BF16)],
        compiler_params=_cparams("parallel"),
        name="mlp",
    )(h2, y2, wo, bo.reshape(1, D), g.reshape(1, D), w1_all, w2_all, gf.reshape(1, D))


def _rwkv_proj_kernel(has_vres, tm, *refs):
    if has_vres:
        (h_ref, hp_ref, gn_ref, mu_ref, vec_ref, wr_ref, wk_ref, wv_ref, w1_ref, w2_ref, a1_ref, a2_ref,
         g1_ref, g2_ref, bd_ref, v1_ref, v2_ref, vf_ref,
         r_out, lw_out, k_out, v_out, a_out, b_out, g_out) = refs
    else:
        (h_ref, hp_ref, gn_ref, mu_ref, vec_ref, wr_ref, wk_ref, wv_ref, w1_ref, w2_ref, a1_ref, a2_ref,
         g1_ref, g2_ref, bd_ref,
         r_out, lw_out, k_out, v_out, a_out, b_out, g_out) = refs
    i = pl.program_id(1)
    gn = gn_ref[...]
    mu = mu_ref[...]
    vec = vec_ref[...]
    w0, a0, k_k, k_a = vec[0:1], vec[1:2], vec[2:3], vec[3:4]
    prev = _rms(hp_ref[0], gn)[SUBLANES - 1:SUBLANES, :]
    prev = jnp.where(i > 0, prev, 0.0)
    nrows = tm // RWKV_PROJ_SPLIT
    row = lax.broadcasted_iota(jnp.int32, (nrows, 1), 0)
    for s in range(RWKV_PROJ_SPLIT):
        rows = slice(s * nrows, (s + 1) * nrows)
        u = _rms(h_ref[0, rows, :], gn)
        shifted = jnp.where(row == 0, prev, pltpu.roll(u, 1, axis=0))
        prev = u[nrows - 1:nrows, :]
        xx = shifted - u

        def mix(j):
            return (u + xx * mu[j:j + 1, :]).astype(BF16)

        xr, xw, xk, xv, xa, xg = [mix(j) for j in range(6)]
        lo_w = _dot(xw, w1_ref[...])
        lo_a = _dot(xa, a1_ref[...])
        lo_g = _dot(xg, g1_ref[...])
        if has_vres:
            lo_v = _dot(xv, v1_ref[...])
        wl = w0 + _dot(jnp.tanh(lo_w).astype(BF16), w2_ref[...])
        a_pre = a0 + _dot(lo_a.astype(BF16), a2_ref[...])
        g = _dot(_sigmoid(lo_g).astype(BF16), g2_ref[...])
        if has_vres:
            gate_pre = vec[4:5] + _dot(lo_v.astype(BF16), v2_ref[...])
        k = _dot(xk, wk_ref[...])
        v = _dot(xv, wv_ref[...])
        r = _dot(xr, wr_ref[...])
        w = -_softplus(-wl) - 0.5
        a = _sigmoid(a_pre)
        if has_vres:
            v = v + (vf_ref[0, rows, :] - v) * _sigmoid(gate_pre)
        kk = k * k_k
        ss = _head_sum(kk * kk, bd_ref[...])
        kk = kk * lax.rsqrt(jnp.maximum(ss, 1e-24))
        r_out[0, rows, :] = r
        lw_out[0, rows, :] = -jnp.exp(w)
        k_out[0, rows, :] = k * (1.0 + (a - 1.0) * k_a)
        v_out[0, rows, :] = v
        a_out[0, rows, :] = -kk
        b_out[0, rows, :] = kk * a
        g_out[0, rows, :] = g


def _block_diag_ones(n, blk):
    idx = jnp.arange(n) // blk
    return (idx[:, None] == idx[None, :]).astype(BF16)


def _rwkv_proj(h, gn, mu, vec, ws, vres, tm):
    B, S, D = h.shape
    has_vres = vres is not None
    tok = pl.BlockSpec((1, tm, D), lambda b, i: (b, i, 0))
    prev = pl.BlockSpec((1, SUBLANES, D), lambda b, i: (b, jnp.maximum(i * (tm // SUBLANES) - 1, 0), 0))
    args = [h, h, gn.reshape(1, D), mu, vec] + list(ws) + [_block_diag_ones(MXU_COLS, RWKV_HEAD)]
    specs = [tok, prev, _full((1, D)), _full(mu.shape), _full(vec.shape)] + [_full(w.shape) for w in ws]
    specs.append(_full((MXU_COLS, MXU_COLS)))
    if has_vres:
        v1, v2, v_first = vres
        args += [v1, v2, v_first]
        specs += [_full(v1.shape), _full(v2.shape), tok]
    out = jax.ShapeDtypeStruct((B, S, D), F32)
    return pl.pallas_call(
        functools.partial(_rwkv_proj_kernel, has_vres, tm),
        grid=(B, S // tm),
        in_specs=specs,
        out_specs=[tok] * 7,
        out_shape=[out] * 7,
        compiler_params=_cparams("parallel", "parallel"),
        name="rwkv_proj",
    )(*args)


def _wkv_kernel(r_ref, lw_ref, k_ref, v_ref, a_ref, b_ref, g_ref, vec_ref, tri_ref, bd_ref,
                y_ref, h_scr, y_scr):
    L = WKV_CHUNK
    nb, _, D = r_ref.shape
    npair = D // LANES

    @pl.when(pl.program_id(1) == 0)
    def _():
        h_scr[...] = jnp.zeros_like(h_scr)

    lane = lax.broadcasted_iota(jnp.int32, (1, LANES), 1)
    first = lane < RWKV_HEAD
    t_idx = lax.broadcasted_iota(jnp.int32, (L, LANES), 0)
    s_idx = lax.broadcasted_iota(jnp.int32, (L, LANES), 1) % RWKV_HEAD
    lower = t_idx > s_idx
    lower_incl = t_idx >= s_idx
    eye = (t_idx == s_idx).astype(F32)
    same_head = (lax.broadcasted_iota(jnp.int32, (LANES, LANES), 0) // RWKV_HEAD
                 == lax.broadcasted_iota(jnp.int32, (LANES, LANES), 1) // RWKV_HEAD)

    def stack(x):
        return jnp.concatenate([jnp.where(first, x, 0.0), jnp.where(first, 0.0, x)], axis=0).astype(BF16)

    ar, bk, blkl, v_b, v_s, h0, w_end = [], [], [], [], [], [], []
    for bi in range(nb):
        lw = lw_ref[bi]
        hi = lw.astype(BF16)
        low = (lw - hi.astype(F32)).astype(BF16)
        cum = _dot(tri_ref[...], jnp.concatenate([hi, low], axis=0))
        e_pos = jnp.exp(cum)
        e_neg = jnp.exp(-cum)
        rt = r_ref[bi] * e_pos
        at = a_ref[bi] * jnp.exp(cum - lw)
        kt = k_ref[bi] * e_neg
        bt = b_ref[bi] * e_neg
        we = e_pos[L - 1:L, :]
        kl = kt * we
        bl = bt * we
        v = v_ref[bi]
        for p in range(npair):
            sl = slice(p * LANES, (p + 1) * LANES)
            ar.append(jnp.concatenate([at[:, sl], rt[:, sl]], axis=0).astype(BF16))
            bk.append(jnp.concatenate([stack(bt[:, sl]), stack(kt[:, sl])], axis=0))
            blkl.append(jnp.concatenate([bl[:, sl], kl[:, sl]], axis=0).astype(BF16))
            v_b.append(v[:, sl].astype(BF16))
            v_s.append(stack(v[:, sl]))
            h0.append(h_scr[bi * npair + p])
            w_end.append(we[:, sl])

    sc = [lax.dot_general(x, y, _NT, preferred_element_type=F32) for x, y in zip(ar, bk)]
    m_ab = [jnp.where(lower, s[:L, :LANES], 0.0) for s in sc]
    m_rb = [jnp.where(lower_incl, s[L:, :LANES], 0.0).astype(BF16) for s in sc]
    m_k = [jnp.concatenate([jnp.where(lower, s[:L, LANES:], 0.0), jnp.where(lower_incl, s[L:, LANES:], 0.0)],
                           axis=0).astype(BF16) for s in sc]
    xy = [_dot(jnp.concatenate([x, m], axis=1), jnp.concatenate([h.astype(BF16), vs], axis=0))
          for x, m, h, vs in zip(ar, m_k, h0, v_s)]
    t_inv = [eye + m for m in m_ab]
    a_cur = [_dot(m.astype(BF16), stack(m)) for m in m_ab]
    n = 4
    while n < L:
        res = [_dot(m.astype(BF16), jnp.concatenate([stack(m), stack(t)], axis=1)) for m, t in zip(a_cur, t_inv)]
        t_inv = [t + x[:, LANES:] for t, x in zip(t_inv, res)]
        a_cur = [x[:, :LANES] for x in res]
        n *= 2
    t_inv = [t + _dot(m.astype(BF16), stack(t)) for t, m in zip(t_inv, a_cur)]
    u = [_dot(t.astype(BF16), stack(x[:L])) for t, x in zip(t_inv, xy)]
    ys = [x[L:] + _dot(m, stack(uu)) for x, m, uu in zip(xy, m_rb, u)]
    dh = [lax.dot_general(w, jnp.concatenate([uu.astype(BF16), vb], axis=0), _TN, preferred_element_type=F32)
          for w, uu, vb in zip(blkl, u, v_b)]
    for j in range(nb * npair):
        bi, p = divmod(j, npair)
        y_scr[bi, :, p * LANES:(p + 1) * LANES] = ys[j]
        decay = jnp.transpose(jnp.broadcast_to(w_end[j], (LANES, LANES)))
        h_scr[j] = h0[j] * decay + jnp.where(same_head, dh[j], 0.0)

    vec = vec_ref[...]
    r_k, ln_w, ln_b = vec[0:1], vec[1:2], vec[2:3]
    bd = bd_ref[...]
    inv_n = 1.0 / RWKV_HEAD
    for bi in range(nb):
        y = y_scr[bi]
        d = y - _head_sum(y, bd) * inv_n
        var = _head_sum(d * d, bd) * inv_n
        yn = d * lax.rsqrt(var + GN_EPS) * ln_w + ln_b
        bonus = _head_sum(r_ref[bi] * k_ref[bi] * r_k, bd) * v_ref[bi]
        y_ref[bi] = ((yn + bonus) * g_ref[bi]).astype(y_ref.dtype)


def _wkv(r, lw, k, v, a, b, g, vec, nb):
    B, S, D = r.shape
    L = WKV_CHUNK
    tok = pl.BlockSpec((nb, L, D), lambda bi, c: (bi, c, 0))
    tri = (jnp.arange(L)[:, None] >= jnp.arange(L)[None, :]).astype(BF16)
    tri = jnp.concatenate([tri, tri], axis=1)
    return pl.pallas_call(
        _wkv_kernel,
        grid=(B // nb, S // L),
        in_specs=[tok] * 7 + [_full(vec.shape), _full((L, 2 * L)), _full((MXU_COLS, MXU_COLS))],
        out_specs=tok,
        out_shape=jax.ShapeDtypeStruct((B, S, D), BF16),
        scratch_shapes=[pltpu.VMEM((nb * D // LANES, LANES, LANES), F32), pltpu.VMEM((nb, L, D), F32)],
        compiler_params=_cparams("parallel", "arbitrary"),
        name="wkv",
    )(r, lw, k, v, a, b, g, vec, tri, _block_diag_ones(MXU_COLS, RWKV_HEAD))


def _rwkv_layer(h, gn, mu, wr, wk, wv, w0, w1, w2, a0, a1, a2, g1, g2, k_k, k_a, r_k, lnx_w, lnx_b,
                vres, tm):
    B, S, D = h.shape
    rows = [w0, a0, k_k, k_a]
    extra = None
    if vres is not None:
        v0, v1, v2, v_first = vres
        rows.append(v0)
        extra = (v1.astype(BF16), v2.astype(BF16), v_first)
    vec = jnp.stack(rows + [jnp.zeros_like(w0)] * (SUBLANES - len(rows)))
    mu8 = jnp.concatenate([mu, jnp.zeros((SUBLANES - mu.shape[0], D), F32)])
    ws = [w.astype(BF16) for w in (wr, wk, wv, w1, w2, a1, a2, g1, g2)]
    r, lw, k, v, a, b, g = _rwkv_proj(h, gn, mu8, vec, ws, extra, tm)
    vec2 = jnp.stack([r_k.reshape(D), lnx_w, lnx_b] + [jnp.zeros_like(w0)] * (SUBLANES - 3))
    return _wkv(r, lw, k, v, a, b, g, vec2, _tile(B, WKV_ROWS)), v


def _glu_kernel(h_ref, g_ref, w_ref, b_ref, z_ref):
    C = z_ref.shape[-1]
    halves = _row_halves(h_ref.shape[0])
    us = [_rms(h_ref[r, :], g_ref[...]).astype(BF16) for r in halves]
    zs = [_dot(u, w_ref[...]) for u in us]
    for r, z in zip(halves, zs):
        z = z + b_ref[...]
        z_ref[r, :] = z[:, :C] * _sigmoid(z[:, C:])


def _glu(h2, g, w1, b1, tm):
    T, D = h2.shape
    N = w1.shape[1]
    return pl.pallas_call(
        _glu_kernel,
        grid=(T // tm,),
        in_specs=[pl.BlockSpec((tm, D), lambda i: (i, 0)), _full((1, D)), _full((D, N)), _full((1, N))],
        out_specs=pl.BlockSpec((tm, N // 2), lambda i: (i, 0)),
        out_shape=jax.ShapeDtypeStruct((T, N // 2), F32),
        compiler_params=_cparams("parallel"),
        name="conv_glu",
    )(h2, g.reshape(1, D), w1, b1.reshape(1, N))


_CONV_HALO = 32


_CONV_ROWS = 64
_CONV_COLS = 128


def _dwconv_kernel(tm, z_ref, zp_ref, dw_ref, vec_ref, o_ref, zbuf):
    i = pl.program_id(1)
    C = z_ref.shape[-1]
    zbuf[0:_CONV_HALO, :] = jnp.where(i > 0, zp_ref[0], 0.0)
    zbuf[_CONV_HALO:, :] = z_ref[0]
    vec = vec_ref[...]
    bdw, ln_w, ln_b = vec[0:1], vec[1:2], vec[2:3]
    off = _CONV_HALO - (CONV_WIDTH - 1)
    span = _CONV_ROWS + _CONV_HALO
    for r0 in range(0, tm, _CONV_ROWS):
        parts = []
        for c0 in range(0, C, _CONV_COLS):
            cols = slice(c0, c0 + _CONV_COLS)
            acc = None
            win = zbuf[r0:r0 + span, cols]
            for res in range(SUBLANES):
                shifted = pltpu.roll(win, span - res, axis=0) if res else win
                for o in range(off + (res - off) % SUBLANES, off + CONV_WIDTH, SUBLANES):
                    q = o - res
                    term = shifted[q:q + _CONV_ROWS, :] * dw_ref[o - off:o - off + 1, cols]
                    acc = term if acc is None else acc + term
            parts.append(acc)
        acc = jnp.concatenate(parts, axis=-1) + bdw
        mean = jnp.mean(acc, axis=-1, keepdims=True)
        d = acc - mean
        var = jnp.mean(d * d, axis=-1, keepdims=True)
        y = d * lax.rsqrt(var + NORM_EPS) * ln_w + ln_b
        o_ref[0, r0:r0 + _CONV_ROWS, :] = (y * _sigmoid(y)).astype(o_ref.dtype)


def _dwconv(z, dw, vec, tm):
    B, S, C = z.shape
    tok = pl.BlockSpec((1, tm, C), lambda b, i: (b, i, 0))
    prev = pl.BlockSpec((1, _CONV_HALO, C), lambda b, i: (b, jnp.maximum(i * (tm // _CONV_HALO) - 1, 0), 0))
    return pl.pallas_call(
        functools.partial(_dwconv_kernel, tm),
        grid=(B, S // tm),
        in_specs=[tok, prev, _full(dw.shape), _full(vec.shape)],
        out_specs=tok,
        out_shape=jax.ShapeDtypeStruct((B, S, C), BF16),
        scratch_shapes=[pltpu.VMEM((_CONV_HALO + tm, C), F32)],
        compiler_params=_cparams("parallel", "parallel"),
        name="conv_dw",
    )(z, z, dw, vec)


def _conv_layer(h, gn, w1, b1, dw, bdw, ln_w, ln_b, tm):
    B, S, D = h.shape
    z = _glu(h.reshape(B * S, D), gn, w1.astype(BF16), b1, _tile(B * S, PROJ_ROWS)).reshape(B, S, -1)
    dw32 = jnp.concatenate([dw, jnp.zeros((32 - CONV_WIDTH, dw.shape[1]), F32)])
    vec = jnp.stack([bdw, ln_w, ln_b] + [jnp.zeros_like(bdw)] * (SUBLANES - 3))
    return _dwconv(z, dw32, vec, tm)


def _rope_table_kernel(pos_ref, frq_ref, cos_ref, sin_ref):
    ang = pos_ref[...] * frq_ref[...]
    cos_ref[...] = jnp.cos(ang)
    sin_ref[...] = jnp.sin(ang)


def _rope_tables(positions):
    B, S = positions.shape
    half = QK_ROPE // 2
    per_row = LANES // half
    rows = B * S // per_row
    inv_freq = ROPE_THETA ** (-jnp.arange(0, QK_ROPE, 2, dtype=F32) / QK_ROPE)
    pos = jnp.repeat(positions.astype(F32).reshape(rows, per_row), half, axis=1)
    frq = jnp.tile(inv_freq, per_row).reshape(1, LANES)
    tr = min(rows, 512)
    cos, sin = pl.pallas_call(
        _rope_table_kernel,
        grid=(rows // tr,),
        in_specs=[pl.BlockSpec((tr, LANES), lambda i: (i, 0)), _full((1, LANES))],
        out_specs=[pl.BlockSpec((tr, LANES), lambda i: (i, 0))] * 2,
        out_shape=[jax.ShapeDtypeStruct((rows, LANES), F32)] * 2,
        compiler_params=_cparams("parallel"),
        name="rope_table",
    )(pos, frq)
    cos = cos.reshape(B * S, half)
    sin = sin.reshape(B * S, half)
    cos_t = jnp.tile(cos, (1, LANES // half))
    sin_t = jnp.tile(jnp.concatenate([-sin, sin], axis=1), (1, LANES // QK_ROPE))
    return cos_t, sin_t


def _mla_proj_kernel(scale, h_ref, gn_ref, wd_ref, gq_ref, gkv_ref, wqn_ref, wqr_ref, wqs_ref, wkn_ref, wv_ref,
                     cos_ref, sin_ref, qn_out, qr_out, kn_out, kr_out, v_out):
    halves = _row_halves(h_ref.shape[0])
    us = [_rms(h_ref[r, :], gn_ref[...]).astype(BF16) for r in halves]
    cs = [_dot(u, wd_ref[...]) for u in us]
    o = Q_LORA + KV_LORA
    reps = qr_out.shape[-1] // LANES
    for r, c in zip(halves, cs):
        cq = _rms(c[:, :Q_LORA], gq_ref[...]).astype(BF16)
        ckv = _rms(c[:, Q_LORA:o], gkv_ref[...]).astype(BF16)
        cos = cos_ref[r, :]
        sin = sin_ref[r, :]
        kr_out[r, :] = (c[:, o:o + LANES] * cos + c[:, o + LANES:o + 2 * LANES] * sin).astype(BF16)
        cos_w = jnp.concatenate([cos] * reps, axis=-1)
        sin_w = jnp.concatenate([sin] * reps, axis=-1)
        qn_out[r, :] = (_dot(cq, wqn_ref[...]) * scale).astype(BF16)
        qr = _dot(cq, wqr_ref[...]) * cos_w + _dot(cq, wqs_ref[...]) * sin_w
        qr_out[r, :] = (qr * scale).astype(BF16)
        kn_out[r, :] = _dot(ckv, wkn_ref[...]).astype(BF16)
        v_out[r, :] = _dot(ckv, wv_ref[...]).astype(BF16)


def _swap_halves(w):
    half = QK_ROPE // 2
    return jnp.concatenate([w[..., half:], w[..., :half]], axis=-1)


def _mla_proj(h2, gn, wd, gq, wuq, gkv, wukv, cos_t, sin_t, tm):
    T, D = h2.shape
    H = MLA_HEADS
    npair = H // 2
    HD = H * QK_NOPE
    kr_w = wd[:, Q_LORA + KV_LORA:]
    zpad = jnp.zeros((D, LANES - 2 * QK_ROPE), F32)
    wd2 = jnp.concatenate([wd[:, :Q_LORA + KV_LORA], kr_w, kr_w, zpad,
                           _swap_halves(kr_w), _swap_halves(kr_w), zpad], axis=1).astype(BF16)
    wq = wuq.reshape(Q_LORA, H, QK_NOPE + QK_ROPE)
    wqn = wq[:, :, :QK_NOPE].reshape(Q_LORA, HD).astype(BF16)
    wq_rope = wq[:, :, QK_NOPE:]

    def pair_pad(w):
        w = w.reshape(Q_LORA, npair, 2 * QK_ROPE)
        w = jnp.concatenate([w, jnp.zeros((Q_LORA, npair, LANES - 2 * QK_ROPE), F32)], axis=-1)
        return w.reshape(Q_LORA, npair * LANES).astype(BF16)

    wqr = pair_pad(wq_rope)
    wqs = pair_pad(_swap_halves(wq_rope))
    wkv = wukv.reshape(KV_LORA, H, QK_NOPE + V_DIM)
    wkn = wkv[:, :, :QK_NOPE].reshape(KV_LORA, HD).astype(BF16)
    wv = wkv[:, :, QK_NOPE:].reshape(KV_LORA, H * V_DIM).astype(BF16)
    scale = math.log2(math.e) / math.sqrt(QK_NOPE + QK_ROPE)
    ws = [wd2, gq.reshape(1, -1), gkv.reshape(1, -1), wqn, wqr, wqs, wkn, wv]
    tokw = lambda n: pl.BlockSpec((tm, n), lambda i: (i, 0))
    return pl.pallas_call(
        functools.partial(_mla_proj_kernel, scale),
        grid=(T // tm,),
        in_specs=[tokw(D), _full((1, D))] + [_full(w.shape) for w in ws] + [tokw(LANES), tokw(LANES)],
        out_specs=[tokw(HD), tokw(npair * LANES), tokw(HD), tokw(LANES), tokw(H * V_DIM)],
        out_shape=[jax.ShapeDtypeStruct((T, HD), BF16), jax.ShapeDtypeStruct((T, npair * LANES), BF16),
                   jax.ShapeDtypeStruct((T, HD), BF16), jax.ShapeDtypeStruct((T, LANES), BF16),
                   jax.ShapeDtypeStruct((T, H * V_DIM), BF16)],
        compiler_params=_cparams("parallel"),
        name="mla_proj",
    )(h2, gn.reshape(1, D), *ws, cos_t, sin_t)


def _attn_kernel(tq, qn_ref, qr_ref, kn_ref, kr_ref, v_ref, o_ref):
    S = qn_ref.shape[1]
    lane = lax.broadcasted_iota(jnp.int32, (1, 2 * LANES), 1)
    head0 = (lane < QK_NOPE) | ((lane >= LANES) & (lane < LANES + QK_ROPE))
    head1 = ((lane >= QK_NOPE) & (lane < LANES)) | ((lane >= LANES + QK_ROPE) & (lane < LANES + 2 * QK_ROPE))
    row = lax.broadcasted_iota(jnp.int32, (2 * tq, tq), 0) % tq
    col = lax.broadcasted_iota(jnp.int32, (2 * tq, tq), 1)
    causal = col <= row
    out_lane = lax.broadcasted_iota(jnp.int32, (1, LANES), 1)

    def scores(i):
        q0, n = i * tq, (i + 1) * tq
        qc = jnp.concatenate([qn_ref[0, q0:n, :], qr_ref[0, q0:n, :]], axis=-1)
        zero = jnp.zeros_like(qc)
        qs = jnp.concatenate([jnp.where(head0, qc, zero), jnp.where(head1, qc, zero)], axis=0)
        kc = jnp.concatenate([kn_ref[0, :n, :], kr_ref[0, :n, :]], axis=-1)
        return lax.dot_general(qs, kc, _NT, preferred_element_type=F32)

    nq = S // tq
    ahead = [scores(i) for i in range(min(ATTN_AHEAD, nq))]
    for i in range(nq):
        q0, n = i * tq, (i + 1) * tq
        s = ahead.pop(0)
        if i + ATTN_AHEAD < nq:
            ahead.append(scores(i + ATTN_AHEAD))
        s_diag = jnp.where(causal, s[:, q0:], -1e30)
        s = jnp.concatenate([s[:, :q0], s_diag], axis=-1) if i else s_diag
        m = jnp.max(s, axis=-1, keepdims=True)
        p = jnp.exp2(s - m)
        l = jnp.sum(p, axis=-1, keepdims=True)
        o = _dot(p.astype(BF16), v_ref[0, :n, :]) / l
        o_ref[0, q0:n, :] = jnp.where(out_lane < V_DIM, o[:tq], o[tq:]).astype(o_ref.dtype)


def _attention(qn, qr, kn, kr, v, tq):
    B, S, HD = qn.shape
    npair = HD // LANES
    spec = pl.BlockSpec((1, S, LANES), lambda b, p: (b, 0, p))
    return pl.pallas_call(
        functools.partial(_attn_kernel, tq),
        grid=(B, npair),
        in_specs=[spec, spec, spec, pl.BlockSpec((1, S, LANES), lambda b, p: (b, 0, 0)), spec],
        out_specs=spec,
        out_shape=jax.ShapeDtypeStruct((B, S, HD), BF16),
        compiler_params=_cparams("parallel", "parallel"),
        name="mla_attn",
    )(qn, qr, kn, kr, v)


def _mla_layer(h, positions, gn, wd, gq, wuq, gkv, wukv, tm, tq):
    B, S, D = h.shape
    cos_t, sin_t = _rope_tables(positions)
    qn, qr, kn, kr, v = _mla_proj(h.reshape(B * S, D), gn, wd, gq, wuq, gkv, wukv, cos_t, sin_t,
                                  _tile(B * S, PROJ_ROWS))
    sh = lambda t: t.reshape(B, S, t.shape[-1])
    return _attention(sh(qn), sh(qr), sh(kn), sh(kr), sh(v), tq)


def _tile(n, pref):
    t = min(n, pref)
    while n % t:
        t //= 2
    return t


def kernel(x, positions, norm_mix, norm_ffn, norm_final, rw_mu, rw_wr, rw_wk, rw_wv, rw_wo, rw_w0, rw_w1, rw_w2, rw_a0, rw_a1, rw_a2, rw_g1, rw_g2, rw_kk, rw_ka, rw_rk, rw_lnx_w, rw_lnx_b, rw_v0, rw_v1, rw_v2, cv_w1, cv_b1, cv_dw, cv_bdw, cv_ln_w, cv_ln_b, cv_w2, cv_b2, ml_wd, ml_gq, ml_wuq, ml_gkv, ml_wukv, ml_wo, ff_w1, ff_w2):
    B, S, D = x.shape
    depth = norm_mix.shape[0]
    tm = _tile(S, 512)
    tq = _tile(S, 128)
    tf = _tile(ff_w1.shape[-1], 1024)
    h = x
    v_first = None
    zero_bias = jnp.zeros((D,), F32)
    ff_w1_bf = ff_w1.astype(BF16)
    ff_w2_bf = ff_w2.astype(BF16)
    ia = ib = ic = 0
    for i in range(depth):
        kind = i % 3
        if kind == 0:
            vres = None if ia == 0 else (rw_v0[ia - 1], rw_v1[ia - 1], rw_v2[ia - 1], v_first)
            y, v = _rwkv_layer(h, norm_mix[i], rw_mu[ia], rw_wr[ia], rw_wk[ia], rw_wv[ia],
                               rw_w0[ia], rw_w1[ia], rw_w2[ia], rw_a0[ia], rw_a1[ia], rw_a2[ia],
                               rw_g1[ia], rw_g2[ia], rw_kk[ia], rw_ka[ia], rw_rk[ia],
                               rw_lnx_w[ia], rw_lnx_b[ia], vres, tm)
            wo, bo = rw_wo[ia], zero_bias
            if ia == 0:
                v_first = v
            ia += 1
        elif kind == 1:
            y = _conv_layer(h, norm_mix[i], cv_w1[ib], cv_b1[ib], cv_dw[ib], cv_bdw[ib],
                            cv_ln_w[ib], cv_ln_b[ib], tm)
            wo, bo = cv_w2[ib], cv_b2[ib]
            ib += 1
        else:
            y = _mla_layer(h, positions, norm_mix[i], ml_wd[ic], ml_gq[ic], ml_wuq[ic],
                           ml_gkv[ic], ml_wukv[ic], tm, tq)
            wo, bo = ml_wo[ic], zero_bias
            ic += 1
        h = _mlp(h.reshape(B * S, D), y.reshape(B * S, -1), wo.astype(BF16), bo, norm_ffn[i],
                 ff_w1_bf, ff_w2_bf, i, norm_final, i == depth - 1, _tile(B * S, MLP_ROWS), tf).reshape(B, S, D)
    return h
```

```python
import functools
import math

import jax
import jax.numpy as jnp
from jax import lax
from jax.experimental import pallas as pl
from jax.experimental.pallas import tpu as pltpu

F32 = jnp.float32
BF16 = jnp.bfloat16

NORM_EPS = 1e-5
GN_EPS = 64e-5
RWKV_HEAD = 64
CONV_WIDTH = 31
MLA_HEADS = 16
QK_NOPE = 64
QK_ROPE = 32
V_DIM = 64
Q_LORA = 384
KV_LORA = 256
ROPE_THETA = 10000.0

LANES = 128
SUBLANES = 8
MXU_COLS = 256
WKV_CHUNK = 64
WKV_ROWS = 8
SEQ_ROWS = 512
MLP_ROWS = 1024
MLP_FF_COLS = 1024
PROJ_ROWS = 1024
ATTN_Q_ROWS = 128
ATTN_AHEAD = 1
VMEM_LIMIT_BYTES = 56 * 1024 * 1024

_NT = (((1,), (1,)), ((), ()))
_TN = (((0,), (0,)), ((), ()))


def _cparams(*sem):
    return pltpu.CompilerParams(dimension_semantics=sem, vmem_limit_bytes=VMEM_LIMIT_BYTES)


def _dot(a, b):
    return jnp.dot(a, b, preferred_element_type=F32)


def _rms(x, g):
    ms = jnp.mean(x * x, axis=-1, keepdims=True)
    return x * lax.rsqrt(ms + NORM_EPS) * g


def _sigmoid(x):
    return 1.0 / (1.0 + jnp.exp(-x))


def _softplus(x):
    return jnp.maximum(x, 0.0) + jnp.log(1.0 + jnp.exp(-jnp.abs(x)))


def _head_sum(x, bd):
    outs = []
    for c in range(x.shape[-1] // MXU_COLS):
        outs.append(_dot(x[:, c * MXU_COLS:(c + 1) * MXU_COLS].astype(BF16), bd))
    return jnp.concatenate(outs, axis=-1)


def _row_halves(n):
    return [slice(0, n // 2), slice(n // 2, n)]


def _full(shape):
    n = len(shape)
    return pl.BlockSpec(shape, lambda *_: (0,) * n)


def _mlp_kernel(final_norm, tf, h_ref, y_ref, wo_ref, bo_ref, g_ref, w1_ref, w2_ref, gf_ref, o_ref, a_scr):
    h = h_ref[...] + bo_ref[...] + _dot(y_ref[...], wo_ref[...])
    u = _rms(h, g_ref[...]).astype(BF16)
    for c in range(a_scr.shape[1] // tf):
        cols = slice(c * tf, (c + 1) * tf)
        a = _dot(u, w1_ref[:, cols])
        a_scr[:, cols] = jnp.square(jnp.maximum(a, 0.0)).astype(BF16)
    out = h + _dot(a_scr[...], w2_ref[...])
    if final_norm:
        out = _rms(out, gf_ref[...])
    o_ref[...] = out


def _resident(shape):
    n = len(shape)
    return pl.BlockSpec(shape, lambda *_: (0,) * n, pipeline_mode=pl.Buffered(1))


def _mlp(h2, y2, wo, bo, g, w1_all, w2_all, layer, gf, final_norm, tm, tf):
    T, D = h2.shape
    K = y2.shape[1]
    F = w1_all.shape[2]
    one = pl.Buffered(1)
    return pl.pallas_call(
        functools.partial(_mlp_kernel, final_norm, tf),
        grid=(T // tm,),
        in_specs=[
            pl.BlockSpec((tm, D), lambda i: (i, 0)),
            pl.BlockSpec((tm, K), lambda i: (i, 0)),
            _resident((K, D)),
            _full((1, D)),
            _full((1, D)),
            pl.BlockSpec((None, D, F), lambda i: (layer, 0, 0), pipeline_mode=one),
            pl.BlockSpec((None, F, D), lambda i: (layer, 0, 0), pipeline_mode=one),
            _full((1, D)),
        ],
        out_specs=pl.BlockSpec((tm, D), lambda i: (i, 0)),
        out_shape=jax.ShapeDtypeStruct((T, D), F32),
        scratch_shapes=[pltpu.VMEM((tm, F), BF16)],
        compiler_params=_cparams("parallel"),
        name="mlp",
    )(h2, y2, wo, bo.reshape(1, D), g.reshape(1, D), w1_all, w2_all, gf.reshape(1, D))


def _rwkv_proj_kernel(has_vres, tm, *refs):
    if has_vres:
        (h_ref, hp_ref, gn_ref, mu_ref, vec_ref, wr_ref, wk_ref, wv_ref, w1_ref, w2_ref, a1_ref, a2_ref,
         g1_ref, g2_ref, bd_ref, v1_ref, v2_ref, vf_ref,
         r_out, lw_out, k_out, v_out, a_out, b_out, g_out) = refs
    else:
        (h_ref, hp_ref, gn_ref, mu_ref, vec_ref, wr_ref, wk_ref, wv_ref, w1_ref, w2_ref, a1_ref, a2_ref,
         g1_ref, g2_ref, bd_ref,
         r_out, lw_out, k_out, v_out, a_out, b_out, g_out) = refs
    i = pl.program_id(1)
    gn = gn_ref[...]
    mu = mu_ref[...]
    vec = vec_ref[...]
    w0, a0, k_k, k_a = vec[0:1], vec[1:2], vec[2:3], vec[3:4]
    prev = _rms(hp_ref[0], gn)[SUBLANES - 1:SUBLANES, :]
    prev = jnp.where(i > 0, prev, 0.0)
    u = _rms(h_ref[0], gn)
    row = lax.broadcasted_iota(jnp.int32, (tm, 1), 0)
    shifted = jnp.where(row == 0, prev, pltpu.roll(u, 1, axis=0))
    xx = shifted - u

    def mix(j):
        return (u + xx * mu[j:j + 1, :]).astype(BF16)

    xr, xw, xk, xv, xa, xg = [mix(j) for j in range(6)]
    lo_w = _dot(xw, w1_ref[...])
    lo_a = _dot(xa, a1_ref[...])
    lo_g = _dot(xg, g1_ref[...])
    if has_vres:
        lo_v = _dot(xv, v1_ref[...])
    wl = w0 + _dot(jnp.tanh(lo_w).astype(BF16), w2_ref[...])
    a_pre = a0 + _dot(lo_a.astype(BF16), a2_ref[...])
    g = _dot(_sigmoid(lo_g).astype(BF16), g2_ref[...])
    if has_vres:
        gate_pre = vec[4:5] + _dot(lo_v.astype(BF16), v2_ref[...])
    k = _dot(xk, wk_ref[...])
    v = _dot(xv, wv_ref[...])
    r = _dot(xr, wr_ref[...])
    w = -_softplus(-wl) - 0.5
    a = _sigmoid(a_pre)
    if has_vres:
        v = v + (vf_ref[0] - v) * _sigmoid(gate_pre)
    kk = k * k_k
    ss = _head_sum(kk * kk, bd_ref[...])
    kk = kk * lax.rsqrt(jnp.maximum(ss, 1e-24))
    r_out[0] = r
    lw_out[0] = -jnp.exp(w)
    k_out[0] = k * (1.0 + (a - 1.0) * k_a)
    v_out[0] = v
    a_out[0] = -kk
    b_out[0] = kk * a
    g_out[0] = g


def _block_diag_ones(n, blk):
    idx = jnp.arange(n) // blk
    return (idx[:, None] == idx[None, :]).astype(BF16)


def _rwkv_proj(h, gn, mu, vec, ws, vres, tm):
    B, S, D = h.shape
    has_vres = vres is not None
    tok = pl.BlockSpec((1, tm, D), lambda b, i: (b, i, 0))
    prev = pl.BlockSpec((1, SUBLANES, D), lambda b, i: (b, jnp.maximum(i * (tm // SUBLANES) - 1, 0), 0))
    args = [h, h, gn.reshape(1, D), mu, vec] + list(ws) + [_block_diag_ones(MXU_COLS, RWKV_HEAD)]
    specs = [tok, prev, _full((1, D)), _full(mu.shape), _full(vec.shape)] + [_full(w.shape) for w in ws]
    specs.append(_full((MXU_COLS, MXU_COLS)))
    if has_vres:
        v1, v2, v_first = vres
        args += [v1, v2, v_first]
        specs += [_full(v1.shape), _full(v2.shape), tok]
    out = jax.ShapeDtypeStruct((B, S, D), F32)
    return pl.pallas_call(
        functools.partial(_rwkv_proj_kernel, has_vres, tm),
        grid=(B, S // tm),
        in_specs=specs,
        out_specs=[tok] * 7,
        out_shape=[out] * 7,
        compiler_params=_cparams("parallel", "parallel"),
        name="rwkv_proj",
    )(*args)


def _wkv_kernel(r_ref, lw_ref, k_ref, v_ref, a_ref, b_ref, g_ref, vec_ref, tri_ref, bd_ref,
                y_ref, h_scr, y_scr):
    L = WKV_CHUNK
    nb, _, D = r_ref.shape
    npair = D // LANES

    @pl.when(pl.program_id(1) == 0)
    def _():
        h_scr[...] = jnp.zeros_like(h_scr)

    lane = lax.broadcasted_iota(jnp.int32, (1, LANES), 1)
    first = lane < RWKV_HEAD
    t_idx = lax.broadcasted_iota(jnp.int32, (L, LANES), 0)
    s_idx = lax.broadcasted_iota(jnp.int32, (L, LANES), 1) % RWKV_HEAD
    lower = t_idx > s_idx
    lower_incl = t_idx >= s_idx
    eye = (t_idx == s_idx).astype(F32)
    same_head = (lax.broadcasted_iota(jnp.int32, (LANES, LANES), 0) // RWKV_HEAD
                 == lax.broadcasted_iota(jnp.int32, (LANES, LANES), 1) // RWKV_HEAD)

    def stack(x):
        return jnp.concatenate([jnp.where(first, x, 0.0), jnp.where(first, 0.0, x)], axis=0).astype(BF16)

    ar, bk, blkl, v_b, v_s, h0, w_end = [], [], [], [], [], [], []
    for bi in range(nb):
        lw = lw_ref[bi]
        hi = lw.astype(BF16)
        low = (lw - hi.astype(F32)).astype(BF16)
        cum = _dot(tri_ref[...], jnp.concatenate([hi, low], axis=0))
        e_pos = jnp.exp(cum)
        e_neg = jnp.exp(-cum)
        rt = r_ref[bi] * e_pos
        at = a_ref[bi] * jnp.exp(cum - lw)
        kt = k_ref[bi] * e_neg
        bt = b_ref[bi] * e_neg
        we = e_pos[L - 1:L, :]
        kl = kt * we
        bl = bt * we
        v = v_ref[bi]
        for p in range(npair):
            sl = slice(p * LANES, (p + 1) * LANES)
            ar.append(jnp.concatenate([at[:, sl], rt[:, sl]], axis=0).astype(BF16))
            bk.append(jnp.concatenate([stack(bt[:, sl]), stack(kt[:, sl])], axis=0))
            blkl.append(jnp.concatenate([bl[:, sl], kl[:, sl]], axis=0).astype(BF16))
            v_b.append(v[:, sl].astype(BF16))
            v_s.append(stack(v[:, sl]))
            h0.append(h_scr[bi * npair + p])
            w_end.append(we[:, sl])

    sc = [lax.dot_general(x, y, _NT, preferred_element_type=F32) for x, y in zip(ar, bk)]
    m_ab = [jnp.where(lower, s[:L, :LANES], 0.0) for s in sc]
    m_rb = [jnp.where(lower_incl, s[L:, :LANES], 0.0).astype(BF16) for s in sc]
    m_k = [jnp.concatenate([jnp.where(lower, s[:L, LANES:], 0.0), jnp.where(lower_incl, s[L:, LANES:], 0.0)],
                           axis=0).astype(BF16) for s in sc]
    xy = [_dot(jnp.concatenate([x, m], axis=1), jnp.concatenate([h.astype(BF16), vs], axis=0))
          for x, m, h, vs in zip(ar, m_k, h0, v_s)]
    t_inv = [eye + m for m in m_ab]
    a_cur = [_dot(m.astype(BF16), stack(m)) for m in m_ab]
    n = 4
    while n < L:
        res = [_dot(m.astype(BF16), jnp.concatenate([stack(m), stack(t)], axis=1)) for m, t in zip(a_cur, t_inv)]
        t_inv = [t + x[:, LANES:] for t, x in zip(t_inv, res)]
        a_cur = [x[:, :LANES] for x in res]
        n *= 2
    t_inv = [t + _dot(m.astype(BF16), stack(t)) for t, m in zip(t_inv, a_cur)]
    u = [_dot(t.astype(BF16), stack(x[:L])) for t, x in zip(t_inv, xy)]
    ys = [x[L:] + _dot(m, stack(uu)) for x, m, uu in zip(xy, m_rb, u)]
    dh = [lax.dot_general(w, jnp.concatenate([uu.astype(BF16), vb], axis=0), _TN, preferred_element_type=F32)
          for w, uu, vb in zip(blkl, u, v_b)]
    for j in range(nb * npair):
        bi, p = divmod(j, npair)
        y_scr[bi, :, p * LANES:(p + 1) * LANES] = ys[j]
        decay = jnp.transpose(jnp.broadcast_to(w_end[j], (LANES, LANES)))
        h_scr[j] = h0[j] * decay + jnp.where(same_head, dh[j], 0.0)

    vec = vec_ref[...]
    r_k, ln_w, ln_b = vec[0:1], vec[1:2], vec[2:3]
    bd = bd_ref[...]
    inv_n = 1.0 / RWKV_HEAD
    for bi in range(nb):
        y = y_scr[bi]
        d = y - _head_sum(y, bd) * inv_n
        var = _head_sum(d * d, bd) * inv_n
        yn = d * lax.rsqrt(var + GN_EPS) * ln_w + ln_b
        bonus = _head_sum(r_ref[bi] * k_ref[bi] * r_k, bd) * v_ref[bi]
        y_ref[bi] = ((yn + bonus) * g_ref[bi]).astype(y_ref.dtype)


def _wkv(r, lw, k, v, a, b, g, vec, nb):
    B, S, D = r.shape
    L = WKV_CHUNK
    tok = pl.BlockSpec((nb, L, D), lambda bi, c: (bi, c, 0))
    tri = (jnp.arange(L)[:, None] >= jnp.arange(L)[None, :]).astype(BF16)
    tri = jnp.concatenate([tri, tri], axis=1)
    return pl.pallas_call(
        _wkv_kernel,
        grid=(B // nb, S // L),
        in_specs=[tok] * 7 + [_full(vec.shape), _full((L, 2 * L)), _full((MXU_COLS, MXU_COLS))],
        out_specs=tok,
        out_shape=jax.ShapeDtypeStruct((B, S, D), BF16),
        scratch_shapes=[pltpu.VMEM((nb * D // LANES, LANES, LANES), F32), pltpu.VMEM((nb, L, D), F32)],
        compiler_params=_cparams("parallel", "arbitrary"),
        name="wkv",
    )(r, lw, k, v, a, b, g, vec, tri, _block_diag_ones(MXU_COLS, RWKV_HEAD))


def _rwkv_layer(h, gn, mu, wr, wk, wv, w0, w1, w2, a0, a1, a2, g1, g2, k_k, k_a, r_k, lnx_w, lnx_b,
                vres, tm):
    B, S, D = h.shape
    rows = [w0, a0, k_k, k_a]
    extra = None
    if vres is not None:
        v0, v1, v2, v_first = vres
        rows.append(v0)
        extra = (v1.astype(BF16), v2.astype(BF16), v_first)
    vec = jnp.stack(rows + [jnp.zeros_like(w0)] * (SUBLANES - len(rows)))
    mu8 = jnp.concatenate([mu, jnp.zeros((SUBLANES - mu.shape[0], D), F32)])
    ws = [w.astype(BF16) for w in (wr, wk, wv, w1, w2, a1, a2, g1, g2)]
    r, lw, k, v, a, b, g = _rwkv_proj(h, gn, mu8, vec, ws, extra, tm)
    vec2 = jnp.stack([r_k.reshape(D), lnx_w, lnx_b] + [jnp.zeros_like(w0)] * (SUBLANES - 3))
    return _wkv(r, lw, k, v, a, b, g, vec2, _tile(B, WKV_ROWS)), v


def _glu_kernel(h_ref, g_ref, w_ref, b_ref, z_ref):
    C = z_ref.shape[-1]
    halves = _row_halves(h_ref.shape[0])
    us = [_rms(h_ref[r, :], g_ref[...]).astype(BF16) for r in halves]
    zs = [_dot(u, w_ref[...]) for u in us]
    for r, z in zip(halves, zs):
        z = z + b_ref[...]
        z_ref[r, :] = z[:, :C] * _sigmoid(z[:, C:])


def _glu(h2, g, w1, b1, tm):
    T, D = h2.shape
    N = w1.shape[1]
    return pl.pallas_call(
        _glu_kernel,
        grid=(T // tm,),
        in_specs=[pl.BlockSpec((tm, D), lambda i: (i, 0)), _full((1, D)), _full((D, N)), _full((1, N))],
        out_specs=pl.BlockSpec((tm, N // 2), lambda i: (i, 0)),
        out_shape=jax.ShapeDtypeStruct((T, N // 2), F32),
        compiler_params=_cparams("parallel"),
        name="conv_glu",
    )(h2, g.reshape(1, D), w1, b1.reshape(1, N))


_CONV_HALO = 32


_CONV_ROWS = 64
_CONV_COLS = 128


def _dwconv_kernel(tm, z_ref, zp_ref, dw_ref, vec_ref, o_ref, zbuf):
    i = pl.program_id(1)
    C = z_ref.shape[-1]
    zbuf[0:_CONV_HALO, :] = jnp.where(i > 0, zp_ref[0], 0.0)
    zbuf[_CONV_HALO:, :] = z_ref[0]
    vec = vec_ref[...]
    bdw, ln_w, ln_b = vec[0:1], vec[1:2], vec[2:3]
    off = _CONV_HALO - (CONV_WIDTH - 1)
    span = _CONV_ROWS + _CONV_HALO
    for r0 in range(0, tm, _CONV_ROWS):
        parts = []
        for c0 in range(0, C, _CONV_COLS):
            cols = slice(c0, c0 + _CONV_COLS)
            acc = None
            win = zbuf[r0:r0 + span, cols]
            for res in range(SUBLANES):
                shifted = pltpu.roll(win, span - res, axis=0) if res else win
                for o in range(off + (res - off) % SUBLANES, off + CONV_WIDTH, SUBLANES):
                    q = o - res
                    term = shifted[q:q + _CONV_ROWS, :] * dw_ref[o - off:o - off + 1, cols]
                    acc = term if acc is None else acc + term
            parts.append(acc)
        acc = jnp.concatenate(parts, axis=-1) + bdw
        mean = jnp.mean(acc, axis=-1, keepdims=True)
        d = acc - mean
        var = jnp.mean(d * d, axis=-1, keepdims=True)
        y = d * lax.rsqrt(var + NORM_EPS) * ln_w + ln_b
        o_ref[0, r0:r0 + _CONV_ROWS, :] = (y * _sigmoid(y)).astype(o_ref.dtype)


def _dwconv(z, dw, vec, tm):
    B, S, C = z.shape
    tok = pl.BlockSpec((1, tm, C), lambda b, i: (b, i, 0))
    prev = pl.BlockSpec((1, _CONV_HALO, C), lambda b, i: (b, jnp.maximum(i * (tm // _CONV_HALO) - 1, 0), 0))
    return pl.pallas_call(
        functools.partial(_dwconv_kernel, tm),
        grid=(B, S // tm),
        in_specs=[tok, prev, _full(dw.shape), _full(vec.shape)],
        out_specs=tok,
        out_shape=jax.ShapeDtypeStruct((B, S, C), BF16),
        scratch_shapes=[pltpu.VMEM((_CONV_HALO + tm, C), F32)],
        compiler_params=_cparams("parallel", "parallel"),
        name="conv_dw",
    )(z, z, dw, vec)


def _conv_layer(h, gn, w1, b1, dw, bdw, ln_w, ln_b, tm):
    B, S, D = h.shape
    z = _glu(h.reshape(B * S, D), gn, w1.astype(BF16), b1, _tile(B * S, PROJ_ROWS)).reshape(B, S, -1)
    dw32 = jnp.concatenate([dw, jnp.zeros((_CONV_HALO - CONV_WIDTH, dw.shape[1]), F32)])
    vec = jnp.stack([bdw, ln_w, ln_b] + [jnp.zeros_like(bdw)] * (SUBLANES - 3))
    return _dwconv(z, dw32, vec, tm)


def _rope_table_kernel(pos_ref, frq_ref, cos_ref, sin_ref):
    ang = pos_ref[...] * frq_ref[...]
    cos_ref[...] = jnp.cos(ang)
    sin_ref[...] = jnp.sin(ang)


def _rope_tables(positions):
    B, S = positions.shape
    half = QK_ROPE // 2
    per_row = LANES // half
    rows = B * S // per_row
    inv_freq = ROPE_THETA ** (-jnp.arange(0, QK_ROPE, 2, dtype=F32) / QK_ROPE)
    pos = jnp.repeat(positions.astype(F32).reshape(rows, per_row), half, axis=1)
    frq = jnp.tile(inv_freq, per_row).reshape(1, LANES)
    tr = min(rows, 512)
    cos, sin = pl.pallas_call(
        _rope_table_kernel,
        grid=(rows // tr,),
        in_specs=[pl.BlockSpec((tr, LANES), lambda i: (i, 0)), _full((1, LANES))],
        out_specs=[pl.BlockSpec((tr, LANES), lambda i: (i, 0))] * 2,
        out_shape=[jax.ShapeDtypeStruct((rows, LANES), F32)] * 2,
        compiler_params=_cparams("parallel"),
        name="rope_table",
    )(pos, frq)
    cos = cos.reshape(B * S, half)
    sin = sin.reshape(B * S, half)
    cos_t = jnp.tile(cos, (1, LANES // half))
    sin_t = jnp.tile(jnp.concatenate([-sin, sin], axis=1), (1, LANES // QK_ROPE))
    return cos_t, sin_t


def _mla_proj_kernel(scale, h_ref, gn_ref, wd_ref, gq_ref, gkv_ref, wqn_ref, wqr_ref, wqs_ref, wkn_ref, wv_ref,
                     cos_ref, sin_ref, qn_out, qr_out, kn_out, kr_out, v_out):
    halves = _row_halves(h_ref.shape[0])
    us = [_rms(h_ref[r, :], gn_ref[...]).astype(BF16) for r in halves]
    cs = [_dot(u, wd_ref[...]) for u in us]
    o = Q_LORA + KV_LORA
    reps = qr_out.shape[-1] // LANES
    for r, c in zip(halves, cs):
        cq = _rms(c[:, :Q_LORA], gq_ref[...]).astype(BF16)
        ckv = _rms(c[:, Q_LORA:o], gkv_ref[...]).astype(BF16)
        cos = cos_ref[r, :]
        sin = sin_ref[r, :]
        kr_out[r, :] = (c[:, o:o + LANES] * cos + c[:, o + LANES:o + 2 * LANES] * sin).astype(BF16)
        cos_w = jnp.concatenate([cos] * reps, axis=-1)
        sin_w = jnp.concatenate([sin] * reps, axis=-1)
        qn_out[r, :] = (_dot(cq, wqn_ref[...]) * scale).astype(BF16)
        qr = _dot(cq, wqr_ref[...]) * cos_w + _dot(cq, wqs_ref[...]) * sin_w
        qr_out[r, :] = (qr * scale).astype(BF16)
        kn_out[r, :] = _dot(ckv, wkn_ref[...]).astype(BF16)
        v_out[r, :] = _dot(ckv, wv_ref[...]).astype(BF16)


def _swap_halves(w):
    half = QK_ROPE // 2
    return jnp.concatenate([w[..., half:], w[..., :half]], axis=-1)


def _mla_proj(h2, gn, wd, gq, wuq, gkv, wukv, cos_t, sin_t, tm):
    T, D = h2.shape
    H = MLA_HEADS
    npair = H // 2
    HD = H * QK_NOPE
    kr_w = wd[:, Q_LORA + KV_LORA:]
    zpad = jnp.zeros((D, LANES - 2 * QK_ROPE), F32)
    wd2 = jnp.concatenate([wd[:, :Q_LORA + KV_LORA], kr_w, kr_w, zpad,
                           _swap_halves(kr_w), _swap_halves(kr_w), zpad], axis=1).astype(BF16)
    wq = wuq.reshape(Q_LORA, H, QK_NOPE + QK_ROPE)
    wqn = wq[:, :, :QK_NOPE].reshape(Q_LORA, HD).astype(BF16)
    wq_rope = wq[:, :, QK_NOPE:]

    def pair_pad(w):
        w = w.reshape(Q_LORA, npair, 2 * QK_ROPE)
        w = jnp.concatenate([w, jnp.zeros((Q_LORA, npair, LANES - 2 * QK_ROPE), F32)], axis=-1)
        return w.reshape(Q_LORA, npair * LANES).astype(BF16)

    wqr = pair_pad(wq_rope)
    wqs = pair_pad(_swap_halves(wq_rope))
    wkv = wukv.reshape(KV_LORA, H, QK_NOPE + V_DIM)
    wkn = wkv[:, :, :QK_NOPE].reshape(KV_LORA, HD).astype(BF16)
    wv = wkv[:, :, QK_NOPE:].reshape(KV_LORA, H * V_DIM).astype(BF16)
    scale = math.log2(math.e) / math.sqrt(QK_NOPE + QK_ROPE)
    ws = [wd2, gq.reshape(1, -1), gkv.reshape(1, -1), wqn, wqr, wqs, wkn, wv]
    tokw = lambda n: pl.BlockSpec((tm, n), lambda i: (i, 0))
    return pl.pallas_call(
        functools.partial(_mla_proj_kernel, scale),
        grid=(T // tm,),
        in_specs=[tokw(D), _full((1, D))] + [_full(w.shape) for w in ws] + [tokw(LANES), tokw(LANES)],
        out_specs=[tokw(HD), tokw(npair * LANES), tokw(HD), tokw(LANES), tokw(H * V_DIM)],
        out_shape=[jax.ShapeDtypeStruct((T, HD), BF16), jax.ShapeDtypeStruct((T, npair * LANES), BF16),
                   jax.ShapeDtypeStruct((T, HD), BF16), jax.ShapeDtypeStruct((T, LANES), BF16),
                   jax.ShapeDtypeStruct((T, H * V_DIM), BF16)],
        compiler_params=_cparams("parallel"),
        name="mla_proj",
    )(h2, gn.reshape(1, D), *ws, cos_t, sin_t)


def _attn_kernel(tq, qn_ref, qr_ref, kn_ref, kr_ref, v_ref, o_ref):
    S = qn_ref.shape[1]
    lane = lax.broadcasted_iota(jnp.int32, (1, 2 * LANES), 1)
    head0 = (lane < QK_NOPE) | ((lane >= LANES) & (lane < LANES + QK_ROPE))
    head1 = ((lane >= QK_NOPE) & (lane < LANES)) | ((lane >= LANES + QK_ROPE) & (lane < LANES + 2 * QK_ROPE))
    row = lax.broadcasted_iota(jnp.int32, (2 * tq, tq), 0) % tq
    col = lax.broadcasted_iota(jnp.int32, (2 * tq, tq), 1)
    causal = col <= row
    out_lane = lax.broadcasted_iota(jnp.int32, (1, LANES), 1)

    def scores(i):
        q0, n = i * tq, (i + 1) * tq
        qc = jnp.concatenate([qn_ref[0, q0:n, :], qr_ref[0, q0:n, :]], axis=-1)
        zero = jnp.zeros_like(qc)
        qs = jnp.concatenate([jnp.where(head0, qc, zero), jnp.where(head1, qc, zero)], axis=0)
        kc = jnp.concatenate([kn_ref[0, :n, :], kr_ref[0, :n, :]], axis=-1)
        return lax.dot_general(qs, kc, _NT, preferred_element_type=F32)

    nq = S // tq
    ahead = [scores(i) for i in range(min(ATTN_AHEAD, nq))]
    for i in range(nq):
        q0, n = i * tq, (i + 1) * tq
        s = ahead.pop(0)
        if i + ATTN_AHEAD < nq:
            ahead.append(scores(i + ATTN_AHEAD))
        s_diag = jnp.where(causal, s[:, q0:], -1e30)
        s = jnp.concatenate([s[:, :q0], s_diag], axis=-1) if i else s_diag
        m = jnp.max(s, axis=-1, keepdims=True)
        p = jnp.exp2(s - m)
        l = jnp.sum(p, axis=-1, keepdims=True)
        o = _dot(p.astype(BF16), v_ref[0, :n, :]) / l
        o_ref[0, q0:n, :] = jnp.where(out_lane < V_DIM, o[:tq], o[tq:]).astype(o_ref.dtype)


def _attention(qn, qr, kn, kr, v, tq):
    B, S, HD = qn.shape
    npair = HD // LANES
    spec = pl.BlockSpec((1, S, LANES), lambda b, p: (b, 0, p))
    return pl.pallas_call(
        functools.partial(_attn_kernel, tq),
        grid=(B, npair),
        in_specs=[spec, spec, spec, pl.BlockSpec((1, S, LANES), lambda b, p: (b, 0, 0)), spec],
        out_specs=spec,
        out_shape=jax.ShapeDtypeStruct((B, S, HD), BF16),
        compiler_params=_cparams("parallel", "parallel"),
        name="mla_attn",
    )(qn, qr, kn, kr, v)


def _mla_layer(h, positions, gn, wd, gq, wuq, gkv, wukv, tm, tq):
    B, S, D = h.shape
    cos_t, sin_t = _rope_tables(positions)
    qn, qr, kn, kr, v = _mla_proj(h.reshape(B * S, D), gn, wd, gq, wuq, gkv, wukv, cos_t, sin_t,
                                  _tile(B * S, PROJ_ROWS))
    sh = lambda t: t.reshape(B, S, t.shape[-1])
    return _attention(sh(qn), sh(qr), sh(kn), sh(kr), sh(v), tq)


def _tile(n, pref):
    t = min(n, pref)
    while n % t:
        t //= 2
    return t


def kernel(x, positions, norm_mix, norm_ffn, norm_final, rw_mu, rw_wr, rw_wk, rw_wv, rw_wo, rw_w0, rw_w1, rw_w2, rw_a0, rw_a1, rw_a2, rw_g1, rw_g2, rw_kk, rw_ka, rw_rk, rw_lnx_w, rw_lnx_b, rw_v0, rw_v1, rw_v2, cv_w1, cv_b1, cv_dw, cv_bdw, cv_ln_w, cv_ln_b, cv_w2, cv_b2, ml_wd, ml_gq, ml_wuq, ml_gkv, ml_wukv, ml_wo, ff_w1, ff_w2):
    B, S, D = x.shape
    depth = norm_mix.shape[0]
    tm = _tile(S, SEQ_ROWS)
    tq = _tile(S, ATTN_Q_ROWS)
    tf = _tile(ff_w1.shape[-1], MLP_FF_COLS)
    h = x
    v_first = None
    zero_bias = jnp.zeros((D,), F32)
    ff_w1_bf = ff_w1.astype(BF16)
    ff_w2_bf = ff_w2.astype(BF16)
    ia = ib = ic = 0
    for i in range(depth):
        kind = i % 3
        if kind == 0:
            vres = None if ia == 0 else (rw_v0[ia - 1], rw_v1[ia - 1], rw_v2[ia - 1], v_first)
            y, v = _rwkv_layer(h, norm_mix[i], rw_mu[ia], rw_wr[ia], rw_wk[ia], rw_wv[ia],
                               rw_w0[ia], rw_w1[ia], rw_w2[ia], rw_a0[ia], rw_a1[ia], rw_a2[ia],
                               rw_g1[ia], rw_g2[ia], rw_kk[ia], rw_ka[ia], rw_rk[ia],
                               rw_lnx_w[ia], rw_lnx_b[ia], vres, tm)
            wo, bo = rw_wo[ia], zero_bias
            if ia == 0:
                v_first = v
            ia += 1
        elif kind == 1:
            y = _conv_layer(h, norm_mix[i], cv_w1[ib], cv_b1[ib], cv_dw[ib], cv_bdw[ib],
                            cv_ln_w[ib], cv_ln_b[ib], tm)
            wo, bo = cv_w2[ib], cv_b2[ib]
            ib += 1
        else:
            y = _mla_layer(h, positions, norm_mix[i], ml_wd[ic], ml_gq[ic], ml_wuq[ic],
                           ml_gkv[ic], ml_wukv[ic], tm, tq)
            wo, bo = ml_wo[ic], zero_bias
            ic += 1
        h = _mlp(h.reshape(B * S, D), y.reshape(B * S, -1), wo.astype(BF16), bo, norm_ffn[i],
                 ff_w1_bf, ff_w2_bf, i, norm_final, i == depth - 1, _tile(B * S, MLP_ROWS), tf).reshape(B, S, D)
    return h
```

```python
import functools
import math

import jax
import jax.numpy as jnp
from jax import lax
from jax.experimental import pallas as pl
from jax.experimental.pallas import tpu as pltpu

F32 = jnp.float32
BF16 = jnp.bfloat16

NORM_EPS = 1e-5
GN_EPS = 64e-5
RWKV_HEAD = 64
CONV_WIDTH = 31
MLA_HEADS = 16
QK_NOPE = 64
QK_ROPE = 32
V_DIM = 64
Q_LORA = 384
KV_LORA = 256
ROPE_THETA = 10000.0

LANES = 128
SUBLANES = 8
MXU_COLS = 256
WKV_CHUNK = 64
WKV_ROWS = 8
SEQ_ROWS = 512
MLP_ROWS = 1024
MLP_FF_COLS = 1024
PROJ_ROWS = 1024
ATTN_Q_ROWS = 128
ATTN_AHEAD = 1
VMEM_LIMIT_BYTES = 56 * 1024 * 1024

_NT = (((1,), (1,)), ((), ()))
_TN = (((0,), (0,)), ((), ()))


def _cparams(*sem):
    return pltpu.CompilerParams(dimension_semantics=sem, vmem_limit_bytes=VMEM_LIMIT_BYTES)


def _dot(a, b):
    return jnp.dot(a, b, preferred_element_type=F32)


def _rms(x, g):
    ms = jnp.mean(x * x, axis=-1, keepdims=True)
    return x * lax.rsqrt(ms + NORM_EPS) * g


def _sigmoid(x):
    return 1.0 / (1.0 + jnp.exp(-x))


def _softplus(x):
    return jnp.maximum(x, 0.0) + jnp.log(1.0 + jnp.exp(-jnp.abs(x)))


def _head_sum(x, bd):
    outs = []
    for c in range(x.shape[-1] // MXU_COLS):
        outs.append(_dot(x[:, c * MXU_COLS:(c + 1) * MXU_COLS].astype(BF16), bd))
    return jnp.concatenate(outs, axis=-1)


def _row_halves(n):
    return [slice(0, n // 2), slice(n // 2, n)]


def _full(shape):
    n = len(shape)
    return pl.BlockSpec(shape, lambda *_: (0,) * n)


def _mlp_kernel(final_norm, tf, h_ref, y_ref, wo_ref, bo_ref, g_ref, w1_ref, w2_ref, gf_ref, o_ref, a_scr):
    h = h_ref[...] + bo_ref[...] + _dot(y_ref[...], wo_ref[...])
    u = _rms(h, g_ref[...]).astype(BF16)
    for c in range(a_scr.shape[1] // tf):
        cols = slice(c * tf, (c + 1) * tf)
        a = _dot(u, w1_ref[:, cols])
        a_scr[:, cols] = jnp.square(jnp.maximum(a, 0.0)).astype(BF16)
    out = h + _dot(a_scr[...], w2_ref[...])
    if final_norm:
        out = _rms(out, gf_ref[...])
    o_ref[...] = out


def _resident(shape):
    n = len(shape)
    return pl.BlockSpec(shape, lambda *_: (0,) * n, pipeline_mode=pl.Buffered(1))


def _mlp(h2, y2, wo, bo, g, w1_all, w2_all, layer, gf, final_norm, tm, tf):
    T, D = h2.shape
    K = y2.shape[1]
    F = w1_all.shape[2]
    one = pl.Buffered(1)
    return pl.pallas_call(
        functools.partial(_mlp_kernel, final_norm, tf),
        grid=(T // tm,),
        in_specs=[
            pl.BlockSpec((tm, D), lambda i: (i, 0)),
            pl.BlockSpec((tm, K), lambda i: (i, 0)),
            _resident((K, D)),
            _full((1, D)),
            _full((1, D)),
            pl.BlockSpec((None, D, F), lambda i: (layer, 0, 0), pipeline_mode=one),
            pl.BlockSpec((None, F, D), lambda i: (layer, 0, 0), pipeline_mode=one),
            _full((1, D)),
        ],
        out_specs=pl.BlockSpec((tm, D), lambda i: (i, 0)),
        out_shape=jax.ShapeDtypeStruct((T, D), F32),
        scratch_shapes=[pltpu.VMEM((tm, F), BF16)],
        compiler_params=_cparams("parallel"),
        name="mlp",
    )(h2, y2, wo, bo.reshape(1, D), g.reshape(1, D), w1_all, w2_all, gf.reshape(1, D))


def _rwkv_proj_kernel(has_vres, tm, *refs):
    if has_vres:
        (h_ref, hp_ref, gn_ref, mu_ref, vec_ref, wr_ref, wk_ref, wv_ref, w1_ref, w2_ref, a1_ref, a2_ref,
         g1_ref, g2_ref, bd_ref, v1_ref, v2_ref, vf_ref,
         r_out, lw_out, k_out, v_out, a_out, b_out, g_out) = refs
    else:
        (h_ref, hp_ref, gn_ref, mu_ref, vec_ref, wr_ref, wk_ref, wv_ref, w1_ref, w2_ref, a1_ref, a2_ref,
         g1_ref, g2_ref, bd_ref,
         r_out, lw_out, k_out, v_out, a_out, b_out, g_out) = refs
    i = pl.program_id(1)
    gn = gn_ref[...]
    mu = mu_ref[...]
    vec = vec_ref[...]
    w0, a0, k_k, k_a = vec[0:1], vec[1:2], vec[2:3], vec[3:4]
    prev = _rms(hp_ref[0], gn)[SUBLANES - 1:SUBLANES, :]
    prev = jnp.where(i > 0, prev, 0.0)
    u = _rms(h_ref[0], gn)
    row = lax.broadcasted_iota(jnp.int32, (tm, 1), 0)
    shifted = jnp.where(row == 0, prev, pltpu.roll(u, 1, axis=0))
    xx = shifted - u

    def mix(j):
        return (u + xx * mu[j:j + 1, :]).astype(BF16)

    xr, xw, xk, xv, xa, xg = [mix(j) for j in range(6)]
    lo_w = _dot(xw, w1_ref[...])
    lo_a = _dot(xa, a1_ref[...])
    lo_g = _dot(xg, g1_ref[...])
    if has_vres:
        lo_v = _dot(xv, v1_ref[...])
    wl = w0 + _dot(jnp.tanh(lo_w).astype(BF16), w2_ref[...])
    a_pre = a0 + _dot(lo_a.astype(BF16), a2_ref[...])
    g = _dot(_sigmoid(lo_g).astype(BF16), g2_ref[...])
    if has_vres:
        gate_pre = vec[4:5] + _dot(lo_v.astype(BF16), v2_ref[...])
    k = _dot(xk, wk_ref[...])
    v = _dot(xv, wv_ref[...])
    r = _dot(xr, wr_ref[...])
    w = -_softplus(-wl) - 0.5
    a = _sigmoid(a_pre)
    if has_vres:
        v = v + (vf_ref[0] - v) * _sigmoid(gate_pre)
    kk = k * k_k
    ss = _head_sum(kk * kk, bd_ref[...])
    kk = kk * lax.rsqrt(jnp.maximum(ss, 1e-24))
    r_out[0] = r
    lw_out[0] = -jnp.exp(w)
    k_out[0] = k * (1.0 + (a - 1.0) * k_a)
    v_out[0] = v
    a_out[0] = -kk
    b_out[0] = kk * a
    g_out[0] = g


def _block_diag_ones(n, blk):
    idx = jnp.arange(n) // blk
    return (idx[:, None] == idx[None, :]).astype(BF16)


def _rwkv_proj(h, gn, mu, vec, ws, vres, tm):
    B, S, D = h.shape
    has_vres = vres is not None
    tok = pl.BlockSpec((1, tm, D), lambda b, i: (b, i, 0))
    prev = pl.BlockSpec((1, SUBLANES, D), lambda b, i: (b, jnp.maximum(i * (tm // SUBLANES) - 1, 0), 0))
    args = [h, h, gn.reshape(1, D), mu, vec] + list(ws) + [_block_diag_ones(MXU_COLS, RWKV_HEAD)]
    specs = [tok, prev, _full((1, D)), _full(mu.shape), _full(vec.shape)] + [_full(w.shape) for w in ws]
    specs.append(_full((MXU_COLS, MXU_COLS)))
    if has_vres:
        v1, v2, v_first = vres
        args += [v1, v2, v_first]
        specs += [_full(v1.shape), _full(v2.shape), tok]
    out = jax.ShapeDtypeStruct((B, S, D), F32)
    return pl.pallas_call(
        functools.partial(_rwkv_proj_kernel, has_vres, tm),
        grid=(B, S // tm),
        in_specs=specs,
        out_specs=[tok] * 7,
        out_shape=[out] * 7,
        compiler_params=_cparams("parallel", "parallel"),
        name="rwkv_proj",
    )(*args)


def _wkv_kernel(r_ref, lw_ref, k_ref, v_ref, a_ref, b_ref, g_ref, vec_ref, tri_ref, bd_ref,
                y_ref, h_scr, y_scr):
    L = WKV_CHUNK
    nb, _, D = r_ref.shape
    npair = D // LANES

    @pl.when(pl.program_id(1) == 0)
    def _():
        h_scr[...] = jnp.zeros_like(h_scr)

    lane = lax.broadcasted_iota(jnp.int32, (1, LANES), 1)
    first = lane < RWKV_HEAD
    t_idx = lax.broadcasted_iota(jnp.int32, (L, LANES), 0)
    s_idx = lax.broadcasted_iota(jnp.int32, (L, LANES), 1) % RWKV_HEAD
    lower = t_idx > s_idx
    lower_incl = t_idx >= s_idx
    eye = (t_idx == s_idx).astype(F32)
    same_head = (lax.broadcasted_iota(jnp.int32, (LANES, LANES), 0) // RWKV_HEAD
                 == lax.broadcasted_iota(jnp.int32, (LANES, LANES), 1) // RWKV_HEAD)

    def stack(x):
        return jnp.concatenate([jnp.where(first, x, 0.0), jnp.where(first, 0.0, x)], axis=0).astype(BF16)

    ar, bk, blkl, v_b, v_s, h0, w_end = [], [], [], [], [], [], []
    for bi in range(nb):
        lw = lw_ref[bi]
        hi = lw.astype(BF16)
        low = (lw - hi.astype(F32)).astype(BF16)
        cum = _dot(tri_ref[...], jnp.concatenate([hi, low], axis=0))
        e_pos = jnp.exp(cum)
        e_neg = jnp.exp(-cum)
        rt = r_ref[bi] * e_pos
        at = a_ref[bi] * jnp.exp(cum - lw)
        kt = k_ref[bi] * e_neg
        bt = b_ref[bi] * e_neg
        we = e_pos[L - 1:L, :]
        kl = kt * we
        bl = bt * we
        v = v_ref[bi]
        for p in range(npair):
            sl = slice(p * LANES, (p + 1) * LANES)
            ar.append(jnp.concatenate([at[:, sl], rt[:, sl]], axis=0).astype(BF16))
            bk.append(jnp.concatenate([stack(bt[:, sl]), stack(kt[:, sl])], axis=0))
            blkl.append(jnp.concatenate([bl[:, sl], kl[:, sl]], axis=0).astype(BF16))
            v_b.append(v[:, sl].astype(BF16))
            v_s.append(stack(v[:, sl]))
            h0.append(h_scr[bi * npair + p])
            w_end.append(we[:, sl])

    sc = [lax.dot_general(x, y, _NT, preferred_element_type=F32) for x, y in zip(ar, bk)]
    m_ab = [jnp.where(lower, s[:L, :LANES], 0.0) for s in sc]
    m_rb = [jnp.where(lower_incl, s[L:, :LANES], 0.0).astype(BF16) for s in sc]
    m_k = [jnp.concatenate([jnp.where(lower, s[:L, LANES:], 0.0), jnp.where(lower_incl, s[L:, LANES:], 0.0)],
                           axis=0).astype(BF16) for s in sc]
    xy = [_dot(jnp.concatenate([x, m], axis=1), jnp.concatenate([h.astype(BF16), vs], axis=0))
          for x, m, h, vs in zip(ar, m_k, h0, v_s)]
    t_inv = [eye + m for m in m_ab]
    a_cur = [_dot(m.astype(BF16), stack(m)) for m in m_ab]
    n = 4
    while n < L:
        res = [_dot(m.astype(BF16), jnp.concatenate([stack(m), stack(t)], axis=1)) for m, t in zip(a_cur, t_inv)]
        t_inv = [t + x[:, LANES:] for t, x in zip(t_inv, res)]
        a_cur = [x[:, :LANES] for x in res]
        n *= 2
    t_inv = [t + _dot(m.astype(BF16), stack(t)) for t, m in zip(t_inv, a_cur)]
    u = [_dot(t.astype(BF16), stack(x[:L])) for t, x in zip(t_inv, xy)]
    ys = [x[L:] + _dot(m, stack(uu)) for x, m, uu in zip(xy, m_rb, u)]
    dh = [lax.dot_general(w, jnp.concatenate([uu.astype(BF16), vb], axis=0), _TN, preferred_element_type=F32)
          for w, uu, vb in zip(blkl, u, v_b)]
    for j in range(nb * npair):
        bi, p = divmod(j, npair)
        y_scr[bi, :, p * LANES:(p + 1) * LANES] = ys[j]
        decay = jnp.transpose(jnp.broadcast_to(w_end[j], (LANES, LANES)))
        h_scr[j] = h0[j] * decay + jnp.where(same_head, dh[j], 0.0)

    vec = vec_ref[...]
    r_k, ln_w, ln_b = vec[0:1], vec[1:2], vec[2:3]
    bd = bd_ref[...]
    inv_n = 1.0 / RWKV_HEAD
    for bi in range(nb):
        y = y_scr[bi]
        d = y - _head_sum(y, bd) * inv_n
        var = _head_sum(d * d, bd) * inv_n
        yn = d * lax.rsqrt(var + GN_EPS) * ln_w + ln_b
        bonus = _head_sum(r_ref[bi] * k_ref[bi] * r_k, bd) * v_ref[bi]
        y_ref[bi] = ((yn + bonus) * g_ref[bi]).astype(y_ref.dtype)


def _wkv(r, lw, k, v, a, b, g, vec, nb):
    B, S, D = r.shape
    L = WKV_CHUNK
    tok = pl.BlockSpec((nb, L, D), lambda bi, c: (bi, c, 0))
    tri = (jnp.arange(L)[:, None] >= jnp.arange(L)[None, :]).astype(BF16)
    tri = jnp.concatenate([tri, tri], axis=1)
    return pl.pallas_call(
        _wkv_kernel,
        grid=(B // nb, S // L),
        in_specs=[tok] * 7 + [_full(vec.shape), _full((L, 2 * L)), _full((MXU_COLS, MXU_COLS))],
        out_specs=tok,
        out_shape=jax.ShapeDtypeStruct((B, S, D), BF16),
        scratch_shapes=[pltpu.VMEM((nb * D // LANES, LANES, LANES), F32), pltpu.VMEM((nb, L, D), F32)],
        compiler_params=_cparams("parallel", "arbitrary"),
        name="wkv",
    )(r, lw, k, v, a, b, g, vec, tri, _block_diag_ones(MXU_COLS, RWKV_HEAD))


def _rwkv_layer(h, gn, mu, wr, wk, wv, w0, w1, w2, a0, a1, a2, g1, g2, k_k, k_a, r_k, lnx_w, lnx_b,
                vres, tm):
    B, S, D = h.shape
    rows = [w0, a0, k_k, k_a]
    extra = None
    if vres is not None:
        v0, v1, v2, v_first = vres
        rows.append(v0)
        extra = (v1.astype(BF16), v2.astype(BF16), v_first)
    vec = jnp.stack(rows + [jnp.zeros_like(w0)] * (SUBLANES - len(rows)))
    mu8 = jnp.concatenate([mu, jnp.zeros((SUBLANES - mu.shape[0], D), F32)])
    ws = [w.astype(BF16) for w in (wr, wk, wv, w1, w2, a1, a2, g1, g2)]
    r, lw, k, v, a, b, g = _rwkv_proj(h, gn, mu8, vec, ws, extra, tm)
    vec2 = jnp.stack([r_k.reshape(D), lnx_w, lnx_b] + [jnp.zeros_like(w0)] * (SUBLANES - 3))
    return _wkv(r, lw, k, v, a, b, g, vec2, _tile(B, WKV_ROWS)), v


def _glu_kernel(h_ref, g_ref, w_ref, b_ref, z_ref):
    C = z_ref.shape[-1]
    halves = _row_halves(h_ref.shape[0])
    us = [_rms(h_ref[r, :], g_ref[...]).astype(BF16) for r in halves]
    zs = [_dot(u, w_ref[...]) for u in us]
    for r, z in zip(halves, zs):
        z = z + b_ref[...]
        z_ref[r, :] = z[:, :C] * _sigmoid(z[:, C:])


def _glu(h2, g, w1, b1, tm):
    T, D = h2.shape
    N = w1.shape[1]
    return pl.pallas_call(
        _glu_kernel,
        grid=(T // tm,),
        in_specs=[pl.BlockSpec((tm, D), lambda i: (i, 0)), _full((1, D)), _full((D, N)), _full((1, N))],
        out_specs=pl.BlockSpec((tm, N // 2), lambda i: (i, 0)),
        out_shape=jax.ShapeDtypeStruct((T, N // 2), F32),
        compiler_params=_cparams("parallel"),
        name="conv_glu",
    )(h2, g.reshape(1, D), w1, b1.reshape(1, N))


_CONV_HALO = 32


_CONV_ROWS = 64
_CONV_COLS = 128


def _dwconv_kernel(tm, h_ref, hp_ref, g_ref, w1_ref, b1_ref, dw_ref, vec_ref, o_ref, zbuf):
    i = pl.program_id(1)
    C = o_ref.shape[-1]
    half = tm // 2

    def glu(rows):
        z = _dot(_rms(rows, g_ref[...]).astype(BF16), w1_ref[...]) + b1_ref[...]
        return z[:, :C] * _sigmoid(z[:, C:])

    za = glu(jnp.concatenate([hp_ref[0], h_ref[0, :half, :]], axis=0))
    zb = glu(h_ref[0, half:, :])
    zbuf[0:_CONV_HALO, :] = jnp.where(i > 0, za[:_CONV_HALO], 0.0)
    zbuf[_CONV_HALO:_CONV_HALO + half, :] = za[_CONV_HALO:]
    zbuf[_CONV_HALO + half:, :] = zb
    vec = vec_ref[...]
    bdw, ln_w, ln_b = vec[0:1], vec[1:2], vec[2:3]
    off = _CONV_HALO - (CONV_WIDTH - 1)
    span = _CONV_ROWS + _CONV_HALO
    for r0 in range(0, tm, _CONV_ROWS):
        parts = []
        for c0 in range(0, C, _CONV_COLS):
            cols = slice(c0, c0 + _CONV_COLS)
            acc = None
            win = zbuf[r0:r0 + span, cols]
            for res in range(SUBLANES):
                shifted = pltpu.roll(win, span - res, axis=0) if res else win
                for o in range(off + (res - off) % SUBLANES, off + CONV_WIDTH, SUBLANES):
                    q = o - res
                    term = shifted[q:q + _CONV_ROWS, :] * dw_ref[o - off:o - off + 1, cols]
                    acc = term if acc is None else acc + term
            parts.append(acc)
        acc = jnp.concatenate(parts, axis=-1) + bdw
        mean = jnp.mean(acc, axis=-1, keepdims=True)
        d = acc - mean
        var = jnp.mean(d * d, axis=-1, keepdims=True)
        y = d * lax.rsqrt(var + NORM_EPS) * ln_w + ln_b
        o_ref[0, r0:r0 + _CONV_ROWS, :] = (y * _sigmoid(y)).astype(o_ref.dtype)


def _dwconv(h, g, w1, b1, dw, vec, tm):
    B, S, D = h.shape
    C = w1.shape[1] // 2
    tok = pl.BlockSpec((1, tm, C), lambda b, i: (b, i, 0))
    prev = pl.BlockSpec((1, _CONV_HALO, D), lambda b, i: (b, jnp.maximum(i * (tm // _CONV_HALO) - 1, 0), 0))
    return pl.pallas_call(
        functools.partial(_dwconv_kernel, tm),
        grid=(B, S // tm),
        in_specs=[pl.BlockSpec((1, tm, D), lambda b, i: (b, i, 0)), prev, _full((1, D)), _resident(w1.shape),
                  _full((1, 2 * C)), _full(dw.shape), _full(vec.shape)],
        out_specs=tok,
        out_shape=jax.ShapeDtypeStruct((B, S, C), BF16),
        scratch_shapes=[pltpu.VMEM((_CONV_HALO + tm, C), F32)],
        compiler_params=_cparams("parallel", "parallel"),
        name="conv_dw",
    )(h, h, g.reshape(1, D), w1, b1.reshape(1, 2 * C), dw, vec)


def _conv_layer(h, gn, w1, b1, dw, bdw, ln_w, ln_b, tm):
    B, S, D = h.shape
    dw32 = jnp.concatenate([dw, jnp.zeros((_CONV_HALO - CONV_WIDTH, dw.shape[1]), F32)])
    vec = jnp.stack([bdw, ln_w, ln_b] + [jnp.zeros_like(bdw)] * (SUBLANES - 3))
    return _dwconv(h, gn, w1.astype(BF16), b1, dw32, vec, tm)


def _rope_table_kernel(pos_ref, frq_ref, cos_ref, sin_ref):
    ang = pos_ref[...] * frq_ref[...]
    cos_ref[...] = jnp.cos(ang)
    sin_ref[...] = jnp.sin(ang)


def _rope_tables(positions):
    B, S = positions.shape
    half = QK_ROPE // 2
    per_row = LANES // half
    rows = B * S // per_row
    inv_freq = ROPE_THETA ** (-jnp.arange(0, QK_ROPE, 2, dtype=F32) / QK_ROPE)
    pos = jnp.repeat(positions.astype(F32).reshape(rows, per_row), half, axis=1)
    frq = jnp.tile(inv_freq, per_row).reshape(1, LANES)
    tr = min(rows, 512)
    cos, sin = pl.pallas_call(
        _rope_table_kernel,
        grid=(rows // tr,),
        in_specs=[pl.BlockSpec((tr, LANES), lambda i: (i, 0)), _full((1, LANES))],
        out_specs=[pl.BlockSpec((tr, LANES), lambda i: (i, 0))] * 2,
        out_shape=[jax.ShapeDtypeStruct((rows, LANES), F32)] * 2,
        compiler_params=_cparams("parallel"),
        name="rope_table",
    )(pos, frq)
    cos = cos.reshape(B * S, half)
    sin = sin.reshape(B * S, half)
    cos_t = jnp.tile(cos, (1, LANES // half))
    sin_t = jnp.tile(jnp.concatenate([-sin, sin], axis=1), (1, LANES // QK_ROPE))
    return cos_t, sin_t


def _mla_proj_kernel(scale, h_ref, gn_ref, wd_ref, gq_ref, gkv_ref, wqn_ref, wqr_ref, wqs_ref, wkn_ref, wv_ref,
                     cos_ref, sin_ref, qn_out, qr_out, kn_out, kr_out, v_out):
    halves = _row_halves(h_ref.shape[0])
    us = [_rms(h_ref[r, :], gn_ref[...]).astype(BF16) for r in halves]
    cs = [_dot(u, wd_ref[...]) for u in us]
    o = Q_LORA + KV_LORA
    reps = qr_out.shape[-1] // LANES
    for r, c in zip(halves, cs):
        cq = _rms(c[:, :Q_LORA], gq_ref[...]).astype(BF16)
        ckv = _rms(c[:, Q_LORA:o], gkv_ref[...]).astype(BF16)
        cos = cos_ref[r, :]
        sin = sin_ref[r, :]
        kr_out[r, :] = (c[:, o:o + LANES] * cos + c[:, o + LANES:o + 2 * LANES] * sin).astype(BF16)
        cos_w = jnp.concatenate([cos] * reps, axis=-1)
        sin_w = jnp.concatenate([sin] * reps, axis=-1)
        qn_out[r, :] = (_dot(cq, wqn_ref[...]) * scale).astype(BF16)
        qr = _dot(cq, wqr_ref[...]) * cos_w + _dot(cq, wqs_ref[...]) * sin_w
        qr_out[r, :] = (qr * scale).astype(BF16)
        kn_out[r, :] = _dot(ckv, wkn_ref[...]).astype(BF16)
        v_out[r, :] = _dot(ckv, wv_ref[...]).astype(BF16)


def _swap_halves(w):
    half = QK_ROPE // 2
    return jnp.concatenate([w[..., half:], w[..., :half]], axis=-1)


def _mla_proj(h2, gn, wd, gq, wuq, gkv, wukv, cos_t, sin_t, tm):
    T, D = h2.shape
    H = MLA_HEADS
    npair = H // 2
    HD = H * QK_NOPE
    kr_w = wd[:, Q_LORA + KV_LORA:]
    zpad = jnp.zeros((D, LANES - 2 * QK_ROPE), F32)
    wd2 = jnp.concatenate([wd[:, :Q_LORA + KV_LORA], kr_w, kr_w, zpad,
                           _swap_halves(kr_w), _swap_halves(kr_w), zpad], axis=1).astype(BF16)
    wq = wuq.reshape(Q_LORA, H, QK_NOPE + QK_ROPE)
    wqn = wq[:, :, :QK_NOPE].reshape(Q_LORA, HD).astype(BF16)
    wq_rope = wq[:, :, QK_NOPE:]

    def pair_pad(w):
        w = w.reshape(Q_LORA, npair, 2 * QK_ROPE)
        w = jnp.concatenate([w, jnp.zeros((Q_LORA, npair, LANES - 2 * QK_ROPE), F32)], axis=-1)
        return w.reshape(Q_LORA, npair * LANES).astype(BF16)

    wqr = pair_pad(wq_rope)
    wqs = pair_pad(_swap_halves(wq_rope))
    wkv = wukv.reshape(KV_LORA, H, QK_NOPE + V_DIM)
    wkn = wkv[:, :, :QK_NOPE].reshape(KV_LORA, HD).astype(BF16)
    wv = wkv[:, :, QK_NOPE:].reshape(KV_LORA, H * V_DIM).astype(BF16)
    scale = math.log2(math.e) / math.sqrt(QK_NOPE + QK_ROPE)
    ws = [wd2, gq.reshape(1, -1), gkv.reshape(1, -1), wqn, wqr, wqs, wkn, wv]
    tokw = lambda n: pl.BlockSpec((tm, n), lambda i: (i, 0))
    return pl.pallas_call(
        functools.partial(_mla_proj_kernel, scale),
        grid=(T // tm,),
        in_specs=[tokw(D), _full((1, D))] + [_full(w.shape) for w in ws] + [tokw(LANES), tokw(LANES)],
        out_specs=[tokw(HD), tokw(npair * LANES), tokw(HD), tokw(LANES), tokw(H * V_DIM)],
        out_shape=[jax.ShapeDtypeStruct((T, HD), BF16), jax.ShapeDtypeStruct((T, npair * LANES), BF16),
                   jax.ShapeDtypeStruct((T, HD), BF16), jax.ShapeDtypeStruct((T, LANES), BF16),
                   jax.ShapeDtypeStruct((T, H * V_DIM), BF16)],
        compiler_params=_cparams("parallel"),
        name="mla_proj",
    )(h2, gn.reshape(1, D), *ws, cos_t, sin_t)


def _attn_kernel(tq, qn_ref, qr_ref, kn_ref, kr_ref, v_ref, o_ref):
    S = qn_ref.shape[1]
    lane = lax.broadcasted_iota(jnp.int32, (1, 2 * LANES), 1)
    head0 = (lane < QK_NOPE) | ((lane >= LANES) & (lane < LANES + QK_ROPE))
    head1 = ((lane >= QK_NOPE) & (lane < LANES)) | ((lane >= LANES + QK_ROPE) & (lane < LANES + 2 * QK_ROPE))
    row = lax.broadcasted_iota(jnp.int32, (2 * tq, tq), 0) % tq
    col = lax.broadcasted_iota(jnp.int32, (2 * tq, tq), 1)
    causal = col <= row
    out_lane = lax.broadcasted_iota(jnp.int32, (1, LANES), 1)

    def scores(i):
        q0, n = i * tq, (i + 1) * tq
        qc = jnp.concatenate([qn_ref[0, q0:n, :], qr_ref[0, q0:n, :]], axis=-1)
        zero = jnp.zeros_like(qc)
        qs = jnp.concatenate([jnp.where(head0, qc, zero), jnp.where(head1, qc, zero)], axis=0)
        kc = jnp.concatenate([kn_ref[0, :n, :], kr_ref[0, :n, :]], axis=-1)
        return lax.dot_general(qs, kc, _NT, preferred_element_type=F32)

    nq = S // tq
    ahead = [scores(i) for i in range(min(ATTN_AHEAD, nq))]
    for i in range(nq):
        q0, n = i * tq, (i + 1) * tq
        s = ahead.pop(0)
        if i + ATTN_AHEAD < nq:
            ahead.append(scores(i + ATTN_AHEAD))
        s_diag = jnp.where(causal, s[:, q0:], -1e30)
        s = jnp.concatenate([s[:, :q0], s_diag], axis=-1) if i else s_diag
        m = jnp.max(s, axis=-1, keepdims=True)
        p = jnp.exp2(s - m)
        l = jnp.sum(p, axis=-1, keepdims=True)
        o = _dot(p.astype(BF16), v_ref[0, :n, :]) / l
        o_ref[0, q0:n, :] = jnp.where(out_lane < V_DIM, o[:tq], o[tq:]).astype(o_ref.dtype)


def _attention(qn, qr, kn, kr, v, tq):
    B, S, HD = qn.shape
    npair = HD // LANES
    spec = pl.BlockSpec((1, S, LANES), lambda b, p: (b, 0, p))
    return pl.pallas_call(
        functools.partial(_attn_kernel, tq),
        grid=(B, npair),
        in_specs=[spec, spec, spec, pl.BlockSpec((1, S, LANES), lambda b, p: (b, 0, 0)), spec],
        out_specs=spec,
        out_shape=jax.ShapeDtypeStruct((B, S, HD), BF16),
        compiler_params=_cparams("parallel", "parallel"),
        name="mla_attn",
    )(qn, qr, kn, kr, v)


def _mla_layer(h, positions, gn, wd, gq, wuq, gkv, wukv, tm, tq):
    B, S, D = h.shape
    cos_t, sin_t = _rope_tables(positions)
    qn, qr, kn, kr, v = _mla_proj(h.reshape(B * S, D), gn, wd, gq, wuq, gkv, wukv, cos_t, sin_t,
                                  _tile(B * S, PROJ_ROWS))
    sh = lambda t: t.reshape(B, S, t.shape[-1])
    return _attention(sh(qn), sh(qr), sh(kn), sh(kr), sh(v), tq)


def _tile(n, pref):
    t = min(n, pref)
    while n % t:
        t //= 2
    return t


def kernel(x, positions, norm_mix, norm_ffn, norm_final, rw_mu, rw_wr, rw_wk, rw_wv, rw_wo, rw_w0, rw_w1, rw_w2, rw_a0, rw_a1, rw_a2, rw_g1, rw_g2, rw_kk, rw_ka, rw_rk, rw_lnx_w, rw_lnx_b, rw_v0, rw_v1, rw_v2, cv_w1, cv_b1, cv_dw, cv_bdw, cv_ln_w, cv_ln_b, cv_w2, cv_b2, ml_wd, ml_gq, ml_wuq, ml_gkv, ml_wukv, ml_wo, ff_w1, ff_w2):
    B, S, D = x.shape
    depth = norm_mix.shape[0]
    tm = _tile(S, SEQ_ROWS)
    tq = _tile(S, ATTN_Q_ROWS)
    tf = _tile(ff_w1.shape[-1], MLP_FF_COLS)
    h = x
    v_first = None
    zero_bias = jnp.zeros((D,), F32)
    ff_w1_bf = ff_w1.astype(BF16)
    ff_w2_bf = ff_w2.astype(BF16)
    ia = ib = ic = 0
    for i in range(depth):
        kind = i % 3
        if kind == 0:
            vres = None if ia == 0 else (rw_v0[ia - 1], rw_v1[ia - 1], rw_v2[ia - 1], v_first)
            y, v = _rwkv_layer(h, norm_mix[i], rw_mu[ia], rw_wr[ia], rw_wk[ia], rw_wv[ia],
                               rw_w0[ia], rw_w1[ia], rw_w2[ia], rw_a0[ia], rw_a1[ia], rw_a2[ia],
                               rw_g1[ia], rw_g2[ia], rw_kk[ia], rw_ka[ia], rw_rk[ia],
                               rw_lnx_w[ia], rw_lnx_b[ia], vres, tm)
            wo, bo = rw_wo[ia], zero_bias
            if ia == 0:
                v_first = v
            ia += 1
        elif kind == 1:
            y = _conv_layer(h, norm_mix[i], cv_w1[ib], cv_b1[ib], cv_dw[ib], cv_bdw[ib],
                            cv_ln_w[ib], cv_ln_b[ib], tm)
            wo, bo = cv_w2[ib], cv_b2[ib]
            ib += 1
        else:
            y = _mla_layer(h, positions, norm_mix[i], ml_wd[ic], ml_gq[ic], ml_wuq[ic],
                           ml_gkv[ic], ml_wukv[ic], tm, tq)
            wo, bo = ml_wo[ic], zero_bias
            ic += 1
        h = _mlp(h.reshape(B * S, D), y.reshape(B * S, -1), wo.astype(BF16), bo, norm_ffn[i],
                 ff_w1_bf, ff_w2_bf, i, norm_final, i == depth - 1, _tile(B * S, MLP_ROWS), tf).reshape(B, S, D)
    return h
```
